```python
import math
import jax, jax.numpy as jnp
from jax import lax
import numpy as np

D_MODEL = 1024
BATCH = 8
SEQ = 4096
DEPTH = 1
DEC_BATCH = 8
DEC_SEQ = 8192
PAST_LEN = 128

RET_HEADS = 4
RET_DK = 128
RET_DV = 256
RET_CHUNK = 128
MLA_HEADS = 8
MLA_NOPE = 128
MLA_ROPE = 64
MLA_V = 128
MLA_QK = MLA_NOPE + MLA_ROPE
Q_LORA = 384
KV_LORA = 256
Q_BLOCK = 128
D_FF = 2816
CONV_W = 3
ROPE_BASE = 10000.0
EPS = 1e-6

RET_Q_W = RET_HEADS * RET_DK
RET_V_W = RET_HEADS * RET_DV
MLA_O_W = MLA_HEADS * MLA_V
IN_SPLITS = (RET_Q_W, RET_Q_W, RET_V_W, RET_V_W, Q_LORA, KV_LORA, MLA_ROPE, D_MODEL, D_MODEL)
IN_W = sum(IN_SPLITS)
IN_SPLIT_POINTS = tuple(int(v) for v in np.cumsum(IN_SPLITS)[:-1])

kernel_name = "hybrid_retention_mla_encoder"


def rms_norm(x, g):
    xf = x.astype(jnp.float32)
    y = xf * lax.rsqrt(jnp.mean(xf * xf, axis=-1, keepdims=True) + EPS)
    return (y * g.astype(jnp.float32)).astype(x.dtype)


def head_group_norm(x, g):
    b, s, h, dv = x.shape
    xf = x.astype(jnp.float32)
    mu = jnp.mean(xf, axis=-1, keepdims=True)
    xc = xf - mu
    y = xc * lax.rsqrt(jnp.mean(xc * xc, axis=-1, keepdims=True) + EPS)
    return (y.reshape(b, s, h * dv) * g.astype(jnp.float32)).astype(x.dtype)


def rope(x, pos):
    d = x.shape[-1]
    inv = ROPE_BASE ** (-jnp.arange(0, d, 2, dtype=jnp.float32) / d)
    ang = pos.astype(jnp.float32)[:, None] * inv[None, :]
    cos = jnp.cos(ang)[None, :, None, :].astype(x.dtype)
    sin = jnp.sin(ang)[None, :, None, :].astype(x.dtype)
    x1, x2 = x[..., : d // 2], x[..., d // 2:]
    return jnp.concatenate([x1 * cos - x2 * sin, x1 * sin + x2 * cos], axis=-1)


def retention_one_direction(q, k, v, log_gamma, strict):
    b, s, h, dk = q.shape
    dv = v.shape[-1]
    c = RET_CHUNK
    n = s // c
    idx = jnp.arange(c, dtype=jnp.float32)
    diff = idx[:, None] - idx[None, :]
    mask = (diff > 0) if strict else (diff >= 0)
    dmat = jnp.where(mask[None], jnp.exp(log_gamma[:, None, None] * jnp.maximum(diff, 0.0)[None]), 0.0)
    dmat = dmat.astype(q.dtype)
    q_dec = jnp.exp(log_gamma[None, :] * (idx[:, None] + 1.0)).astype(q.dtype)[None, :, :, None]
    k_dec = jnp.exp(log_gamma[None, :] * (c - 1.0 - idx[:, None])).astype(q.dtype)[None, :, :, None]
    chunk_dec = jnp.exp(log_gamma * c)[None, :, None, None]

    def to_chunks(t):
        return t.reshape(b, n, c, h, t.shape[-1]).transpose(1, 0, 2, 3, 4)

    def step(state, inp):
        qn, kn, vn = inp
        scores = jnp.einsum('bihd,bjhd->bhij', qn, kn) * dmat[None]
        intra = jnp.einsum('bhij,bjhv->bihv', scores, vn)
        cross = jnp.einsum('bihd,bhdv->bihv', qn * q_dec, state.astype(qn.dtype))
        new_state = state * chunk_dec + jnp.einsum('bjhd,bjhv->bhdv', kn * k_dec, vn).astype(jnp.float32)
        return new_state, intra + cross

    state0 = jnp.zeros((b, h, dk, dv), jnp.float32)
    _, out = lax.scan(step, state0, (to_chunks(q), to_chunks(k), to_chunks(v)))
    return out.transpose(1, 0, 2, 3, 4).reshape(b, s, h, dv)


def bidirectional_retention(q, k, v, decay_fwd, decay_bwd):
    lg_f = jax.nn.log_sigmoid(decay_fwd.astype(jnp.float32))
    lg_b = jax.nn.log_sigmoid(decay_bwd.astype(jnp.float32))
    fwd = retention_one_direction(q, k, v, lg_f, False)
    bwd = retention_one_direction(q[:, ::-1], k[:, ::-1], v[:, ::-1], lg_b, True)[:, ::-1]
    return fwd + bwd


def mla_attention(c_q, c_kv, k_rope, pos, g_cq, w_uq, g_ckv, w_ukv, g_qn, g_kn):
    b, s, _ = c_q.shape
    q = (rms_norm(c_q, g_cq) @ w_uq).reshape(b, s, MLA_HEADS, MLA_QK)
    kv = (rms_norm(c_kv, g_ckv) @ w_ukv).reshape(b, s, MLA_HEADS, MLA_NOPE + MLA_V)
    k_nope, v = kv[..., :MLA_NOPE], kv[..., MLA_NOPE:]
    k = jnp.concatenate([k_nope, jnp.broadcast_to(k_rope[:, :, None, :], (b, s, MLA_HEADS, MLA_ROPE))], axis=-1)
    q = rms_norm(q, g_qn)
    k = rms_norm(k, g_kn)
    q = jnp.concatenate([q[..., :MLA_NOPE], rope(q[..., MLA_NOPE:], pos)], axis=-1)
    k = jnp.concatenate([k[..., :MLA_NOPE], rope(k[..., MLA_NOPE:], pos)], axis=-1)
    nb = s // Q_BLOCK
    qb = (q * (MLA_QK ** -0.5)).reshape(b, nb, Q_BLOCK, MLA_HEADS, MLA_QK).transpose(1, 0, 2, 3, 4)

    def attend(q_blk):
        sc = jnp.einsum('bqhd,bkhd->bhqk', q_blk, k).astype(jnp.float32)
        p = jax.nn.softmax(sc, axis=-1).astype(v.dtype)
        return jnp.einsum('bhqk,bkhv->bqhv', p, v)

    o = lax.map(attend, qb)
    return o.transpose(1, 0, 2, 3, 4).reshape(b, s, MLA_O_W)


def depthwise_conv3(u, w, bias):
    up = jnp.pad(u, ((0, 0), (1, 1), (0, 0)))
    return up[:, :-2] * w[0] + up[:, 1:-1] * w[1] + up[:, 2:] * w[2] + bias


def encoder_trunk(x, g_mix, w_in, ret_decay_fwd, ret_decay_bwd, ret_gn_g, w_ret_o,
                  g_cq, w_uq, g_ckv, w_ukv, g_qn, g_kn, w_mla_o, w_out,
                  g_ffn, w_up, conv_w, conv_b, w_down):
    b, s, _ = x.shape
    pos = jnp.arange(s)
    for l in range(DEPTH):
        h = rms_norm(x, g_mix[l])
        proj = h @ w_in[l]
        rq, rk, rv, rg, cq, ckv, kr, gate_r, gate_a = jnp.split(proj, IN_SPLIT_POINTS, axis=-1)
        rq = rope(rq.reshape(b, s, RET_HEADS, RET_DK), pos)
        rk = rope(rk.reshape(b, s, RET_HEADS, RET_DK), pos) * (RET_DK ** -0.5)
        rv = rv.reshape(b, s, RET_HEADS, RET_DV)
        ret = bidirectional_retention(rq, rk, rv, ret_decay_fwd[l], ret_decay_bwd[l])
        ret_branch = (jax.nn.silu(rg) * head_group_norm(ret, ret_gn_g[l])) @ w_ret_o[l]
        kr = rope(kr[:, :, None, :], pos)[:, :, 0, :]
        mla = mla_attention(cq, ckv, kr, pos, g_cq[l], w_uq[l], g_ckv[l], w_ukv[l], g_qn[l], g_kn[l])
        mla_branch = mla @ w_mla_o[l]
        merged = jax.nn.sigmoid(gate_r) * ret_branch + jax.nn.sigmoid(gate_a) * mla_branch
        x = x + merged @ w_out[l]
        h = rms_norm(x, g_ffn[l])
        u = depthwise_conv3(h @ w_up[l], conv_w[l], conv_b[l])
        x = x + (jax.nn.silu(u[..., :D_FF]) * u[..., D_FF:]) @ w_down[l]
    return x


def setup_inputs(seed: int = 0) -> dict:
    key = jax.random.key(seed)
    ks = jax.random.split(key, 24)
    f32 = jnp.float32

    def nrm(k, shape, scale):
        return jax.random.normal(k, shape, f32) * scale

    def gain(k, shape):
        return 1.0 + 0.02 * jax.random.normal(k, shape, f32)

    decay_base = jnp.log(2.0 ** (5.0 + jnp.arange(RET_HEADS, dtype=f32)) - 1.0)
    return {
        "x_prompt": jax.random.normal(ks[0], (BATCH, SEQ, D_MODEL), f32),
        "x_sample": jax.random.normal(ks[1], (DEC_BATCH, DEC_SEQ, D_MODEL), f32),
        "g_mix": gain(ks[2], (DEPTH, D_MODEL)),
        "w_in": nrm(ks[3], (DEPTH, D_MODEL, IN_W), D_MODEL ** -0.5),
        "ret_decay_fwd": decay_base[None] + 0.1 * jax.random.normal(ks[4], (DEPTH, RET_HEADS), f32),
        "ret_decay_bwd": decay_base[None] + 0.1 * jax.random.normal(ks[5], (DEPTH, RET_HEADS), f32),
        "ret_gn_g": gain(ks[6], (DEPTH, RET_V_W)),
        "w_ret_o": nrm(ks[7], (DEPTH, RET_V_W, D_MODEL), RET_V_W ** -0.5),
        "g_cq": gain(ks[8], (DEPTH, Q_LORA)),
        "w_uq": nrm(ks[9], (DEPTH, Q_LORA, MLA_HEADS * MLA_QK), Q_LORA ** -0.5),
        "g_ckv": gain(ks[10], (DEPTH, KV_LORA)),
        "w_ukv": nrm(ks[11], (DEPTH, KV_LORA, MLA_HEADS * (MLA_NOPE + MLA_V)), KV_LORA ** -0.5),
        "g_qn": gain(ks[12], (DEPTH, MLA_QK)),
        "g_kn": gain(ks[13], (DEPTH, MLA_QK)),
        "w_mla_o": nrm(ks[14], (DEPTH, MLA_O_W, D_MODEL), MLA_O_W ** -0.5),
        "w_out": nrm(ks[15], (DEPTH, D_MODEL, D_MODEL), D_MODEL ** -0.5),
        "g_ffn": gain(ks[16], (DEPTH, D_MODEL)),
        "w_up": nrm(ks[17], (DEPTH, D_MODEL, 2 * D_FF), D_MODEL ** -0.5),
        "conv_w": nrm(ks[18], (DEPTH, CONV_W, 2 * D_FF), CONV_W ** -0.5),
        "conv_b": nrm(ks[19], (DEPTH, 2 * D_FF), 0.02),
        "w_down": nrm(ks[20], (DEPTH, D_FF, D_MODEL), D_FF ** -0.5),
    }


def reference(x_prompt, x_sample, g_mix, w_in, ret_decay_fwd, ret_decay_bwd, ret_gn_g, w_ret_o,
              g_cq, w_uq, g_ckv, w_ukv, g_qn, g_kn, w_mla_o, w_out,
              g_ffn, w_up, conv_w, conv_b, w_down):
    weights = (g_mix, w_in, ret_decay_fwd, ret_decay_bwd, ret_gn_g, w_ret_o,
               g_cq, w_uq, g_ckv, w_ukv, g_qn, g_kn, w_mla_o, w_out,
               g_ffn, w_up, conv_w, conv_b, w_down)
    y_prompt = encoder_trunk(x_prompt, *weights)
    y_sample = encoder_trunk(x_sample, *weights)
    return (y_prompt, y_sample)
```

```python
import functools

import jax
import jax.numpy as jnp
from jax import lax
from jax.experimental import pallas as pl
from jax.experimental.pallas import tpu as pltpu

D_MODEL = 1024
RET_HEADS = 4
RET_DK = 128
RET_DV = 256
RET_CHUNK = 128
MLA_HEADS = 8
MLA_NOPE = 128
MLA_ROPE = 64
MLA_V = 128
MLA_QK = MLA_NOPE + MLA_ROPE
MLA_QK_PAD = 256
Q_LORA = 384
KV_LORA = 256
D_FF = 2816
ROPE_BASE = 10000.0
EPS = 1e-6

RET_Q_W = RET_HEADS * RET_DK
RET_V_W = RET_HEADS * RET_DV
LATENT_W = Q_LORA + KV_LORA + 128
IN_W_PAD = 2 * RET_Q_W + 2 * RET_V_W + LATENT_W + 2 * D_MODEL

LANES = 128
VMEM_LIMIT_BYTES = 56 * 1024 * 1024

F32 = jnp.float32
BF16 = jnp.bfloat16


def _params(*semantics):
    return pltpu.CompilerParams(dimension_semantics=semantics, vmem_limit_bytes=VMEM_LIMIT_BYTES)


def _resident(shape):
    return pl.BlockSpec(shape, lambda *_: (0,) * len(shape), pipeline_mode=pl.Buffered(1))


def _sigmoid(x):
    return 1.0 / (1.0 + jnp.exp(-x))


def _dot(a, b):
    return jnp.dot(a, b, preferred_element_type=F32)


def _dot_nt(a, b):
    return lax.dot_general(a, b, (((1,), (1,)), ((), ())), preferred_element_type=F32)


def _dot_tn(a, b):
    return lax.dot_general(a, b, (((0,), (0,)), ((), ())), preferred_element_type=F32)


def _inproj_kernel(x_ref, g_ref, w_ref, cos_ref, sin_ref,
                   qk_ref, v_ref, rg_ref, lat_ref, gates_ref):
    xf = x_ref[...]
    ms = jnp.mean(xf * xf, axis=-1, keepdims=True)
    h = (xf * lax.rsqrt(ms + EPS) * g_ref[...]).astype(BF16)
    cos = cos_ref[...]
    sin = sin_ref[...]

    def proj(c0, width):
        return _dot(h, w_ref[:, c0:c0 + width])

    for seg, scale in ((0, None), (1, RET_DK ** -0.5)):
        acc = proj(seg * RET_Q_W, RET_Q_W)
        for hd in range(RET_HEADS):
            xs = acc[:, hd * RET_DK:(hd + 1) * RET_DK]
            r = xs * cos + pltpu.roll(xs, RET_DK // 2, 1) * sin
            if scale is not None:
                r = r * scale
            c0 = seg * RET_Q_W + hd * RET_DK
            qk_ref[:, c0:c0 + RET_DK] = r.astype(BF16)
    base = 2 * RET_Q_W
    half = RET_V_W // 2
    for j in range(2):
        v_ref[:, j * half:(j + 1) * half] = proj(base + j * half, half).astype(BF16)
    base += RET_V_W
    for j in range(2):
        a = proj(base + j * half, half)
        rg_ref[:, j * half:(j + 1) * half] = (a * _sigmoid(a)).astype(BF16)
    base += RET_V_W
    lat_ref[...] = proj(base, LATENT_W).astype(BF16)
    base += LATENT_W
    for j in range(4):
        a = proj(base + j * half, half)
        gates_ref[:, j * half:(j + 1) * half] = _sigmoid(a).astype(BF16)


def _inproj(x2, g_mix, w_in_p, cos_r, sin_r, seq, tm):
    t = x2.shape[0]
    nseq = seq // tm
    row = lambda w: pl.BlockSpec((tm, w), lambda i: (i, 0))
    pos = pl.BlockSpec((tm, LANES), lambda i: (i % nseq, 0))
    out_w = (2 * RET_Q_W, RET_V_W, RET_V_W, LATENT_W, 2 * D_MODEL)
    return pl.pallas_call(
        _inproj_kernel,
        grid=(t // tm,),
        in_specs=[row(D_MODEL), _resident((1, D_MODEL)), _resident((D_MODEL, IN_W_PAD)), pos, pos],
        out_specs=[row(w) for w in out_w],
        out_shape=[jax.ShapeDtypeStruct((t, w), BF16) for w in out_w],
        compiler_params=_params("arbitrary"),
        name="inproj",
    )(x2, g_mix, w_in_p, cos_r, sin_r)


_T_DMAT, _T_QF, _T_QB, _T_KF, _T_KB, _T_CF, _T_CB = range(7)


def _retention_kernel(dec_ref, gn_ref, qk_ref, v_ref, rg_ref, o_ref,
                      tab_ref, sf_ref, sb_ref, sball_ref, *, ts, nblk):
    c = RET_CHUNK
    ncb = ts // c
    b = pl.program_id(0)
    ph = pl.program_id(1)
    i = pl.program_id(2)

    @pl.when((b == 0) & (ph == 0) & (i == 0))
    def _tables():
        row = lax.broadcasted_iota(jnp.int32, (c, c), 0).astype(F32)
        col = lax.broadcasted_iota(jnp.int32, (c, c), 1).astype(F32)
        diff = row - col
        for hd in range(RET_HEADS):
            def log_sigmoid(d):
                return jnp.minimum(d, 0.0) - jnp.log1p(jnp.exp(-jnp.abs(d)))
            lgf = log_sigmoid(dec_ref[0, hd:hd + 1, :])
            lgb = log_sigmoid(dec_ref[1, hd:hd + 1, :])
            tab_ref[hd, _T_DMAT] = jnp.where(diff >= 0.0,
                                             jnp.exp(lgf * jnp.maximum(diff, 0.0)),
                                             jnp.exp(lgb * jnp.maximum(-diff, 0.0)))
            tab_ref[hd, _T_QF] = jnp.exp(lgf * (row + 1.0))
            tab_ref[hd, _T_QB] = jnp.exp(lgb * (c - row))
            tab_ref[hd, _T_KF] = jnp.exp(lgf * (c - 1.0 - row))
            tab_ref[hd, _T_KB] = jnp.exp(lgb * row)
            tab_ref[hd, _T_CF] = jnp.exp(lgf * (row * 0.0 + c))
            tab_ref[hd, _T_CB] = jnp.exp(lgb * (row * 0.0 + c))

    def chunk_decay(hd, which):
        t = tab_ref[hd, which]
        return jnp.concatenate([t, t], axis=1)

    @pl.when(ph == 0)
    def _backward_states():
        @pl.when(i == 0)
        def _():
            sb_ref[...] = jnp.zeros_like(sb_ref)
        blk = nblk - 1 - i
        for cb in reversed(range(ncb)):
            r0 = cb * c
            for hd in range(RET_HEADS):
                k = qk_ref[0, r0:r0 + c, RET_Q_W + hd * RET_DK:RET_Q_W + (hd + 1) * RET_DK]
                v = v_ref[0, r0:r0 + c, hd * RET_DV:(hd + 1) * RET_DV]
                st = sb_ref[hd]
                sball_ref[blk * ncb + cb, hd] = st.astype(BF16)
                kd = (k.astype(F32) * tab_ref[hd, _T_KB]).astype(BF16)
                sb_ref[hd] = st * chunk_decay(hd, _T_CB) + _dot_tn(kd, v)

    @pl.when(ph == 1)
    def _forward():
        @pl.when(i == 0)
        def _():
            sf_ref[...] = jnp.zeros_like(sf_ref)
        for cb in range(ncb):
            r0 = cb * c
            for hd in range(RET_HEADS):
                q = qk_ref[0, r0:r0 + c, hd * RET_DK:(hd + 1) * RET_DK]
                k = qk_ref[0, r0:r0 + c, RET_Q_W + hd * RET_DK:RET_Q_W + (hd + 1) * RET_DK]
                v = v_ref[0, r0:r0 + c, hd * RET_DV:(hd + 1) * RET_DV]
                qf32 = q.astype(F32)
                kf32 = k.astype(F32)
                sd = (_dot_nt(q, k) * tab_ref[hd, _T_DMAT]).astype(BF16)
                st = sf_ref[hd]
                out = _dot(sd, v)
                out = out + _dot((qf32 * tab_ref[hd, _T_QF]).astype(BF16), st.astype(BF16))
                out = out + _dot((qf32 * tab_ref[hd, _T_QB]).astype(BF16), sball_ref[i * ncb + cb, hd])
                kd = (kf32 * tab_ref[hd, _T_KF]).astype(BF16)
                sf_ref[hd] = st * chunk_decay(hd, _T_CF) + _dot_tn(kd, v)
                mu = jnp.mean(out, axis=-1, keepdims=True)
                xc = out - mu
                var = jnp.mean(xc * xc, axis=-1, keepdims=True)
                y = xc * lax.rsqrt(var + EPS) * gn_ref[:, hd * RET_DV:(hd + 1) * RET_DV]
                gate = rg_ref[0, r0:r0 + c, hd * RET_DV:(hd + 1) * RET_DV].astype(F32)
                o_ref[0, r0:r0 + c, hd * RET_DV:(hd + 1) * RET_DV] = (y * gate).astype(BF16)


def _retention(dec, gn_g, qk, v, rg, ts):
    bsz, seq, _ = qk.shape
    nblk = seq // ts
    nchunk = seq // RET_CHUNK
    sweep = lambda b, ph, i: (b, i * ph + (nblk - 1 - i) * (1 - ph), 0)
    fwd_only = lambda b, ph, i: (b, i * ph, 0)
    return pl.pallas_call(
        functools.partial(_retention_kernel, ts=ts, nblk=nblk),
        grid=(bsz, 2, nblk),
        in_specs=[_resident((2, RET_HEADS, LANES)), _resident((1, RET_V_W)),
                  pl.BlockSpec((1, ts, 2 * RET_Q_W), sweep),
                  pl.BlockSpec((1, ts, RET_V_W), sweep),
                  pl.BlockSpec((1, ts, RET_V_W), fwd_only)],
        out_specs=pl.BlockSpec((1, ts, RET_V_W), fwd_only),
        out_shape=jax.ShapeDtypeStruct((bsz, seq, RET_V_W), BF16),
        scratch_shapes=[pltpu.VMEM((RET_HEADS, 7, RET_CHUNK, RET_CHUNK), F32),
                        pltpu.VMEM((RET_HEADS, RET_DK, RET_DV), F32),
                        pltpu.VMEM((RET_HEADS, RET_DK, RET_DV), F32),
                        pltpu.VMEM((nchunk, RET_HEADS, RET_DK, RET_DV), BF16)],
        compiler_params=_params("arbitrary", "arbitrary", "arbitrary"),
        name="retention",
    )(dec, gn_g, qk, v, rg)


def _rope64(x, cos, sin_lo, sin_hi):
    return x * cos + pltpu.roll(x, LANES - MLA_ROPE // 2, 1) * sin_lo + pltpu.roll(x, MLA_ROPE // 2, 1) * sin_hi


def _mla_prep_kernel(lat_ref, gcq_ref, wuq_ref, gckv_ref, wukv_ref, gqn_ref, gkn_ref,
                     cos_ref, slo_ref, shi_ref, q_ref, k_ref, v_ref):
    cos = cos_ref[...]
    slo = slo_ref[...]
    shi = shi_ref[...]

    def rms(x, g):
        ms = jnp.mean(x * x, axis=-1, keepdims=True)
        return x * lax.rsqrt(ms + EPS) * g

    cq = lat_ref[0, :, :Q_LORA].astype(F32)
    ckv = lat_ref[0, :, Q_LORA:Q_LORA + KV_LORA].astype(F32)
    kr = lat_ref[0, :, Q_LORA + KV_LORA:].astype(F32)
    cqn = rms(cq, gcq_ref[...]).astype(BF16)
    ckvn = rms(ckv, gckv_ref[...]).astype(BF16)
    gqn = gqn_ref[...]
    gkn = gkn_ref[...]
    kr = _rope64(kr, cos, slo, shi)
    kr_sq = jnp.sum(kr * kr, axis=-1, keepdims=True)
    kr_rot = _rope64(kr * gkn[:, MLA_NOPE:], cos, slo, shi)
    qscale = MLA_QK ** -0.5
    for hd in range(MLA_HEADS):
        c0 = hd * MLA_QK_PAD
        qh = _dot(cqn, wuq_ref[:, c0:c0 + MLA_QK_PAD])
        ssq = jnp.sum(qh * qh, axis=-1, keepdims=True) * (1.0 / MLA_QK)
        qn = qh * lax.rsqrt(ssq + EPS) * gqn
        q_ref[0, hd, :, :MLA_NOPE] = (qn[:, :MLA_NOPE] * qscale).astype(BF16)
        q_ref[0, hd, :, MLA_NOPE:] = (_rope64(qn[:, MLA_NOPE:], cos, slo, shi) * qscale).astype(BF16)
        kvh = _dot(ckvn, wukv_ref[:, c0:c0 + MLA_QK_PAD])
        kn = kvh[:, :MLA_NOPE]
        ssk = (jnp.sum(kn * kn, axis=-1, keepdims=True) + kr_sq) * (1.0 / MLA_QK)
        rk = lax.rsqrt(ssk + EPS)
        k_ref[0, hd, :, :MLA_NOPE] = (kn * rk * gkn[:, :MLA_NOPE]).astype(BF16)
        k_ref[0, hd, :, MLA_NOPE:] = (kr_rot * rk).astype(BF16)
        v_ref[0, hd] = kvh[:, MLA_NOPE:].astype(BF16)


def _mla_prep(lat, g_cq, w_uq_p, g_ckv, w_ukv, g_qn_p, g_kn_p, cos_m, slo_m, shi_m, ts):
    bsz, seq, _ = lat.shape
    pos = pl.BlockSpec((ts, LANES), lambda b, i: (i, 0))
    head_blk = lambda w: pl.BlockSpec((1, MLA_HEADS, ts, w), lambda b, i: (b, 0, i, 0))
    return pl.pallas_call(
        _mla_prep_kernel,
        grid=(bsz, seq // ts),
        in_specs=[pl.BlockSpec((1, ts, LATENT_W), lambda b, i: (b, i, 0)),
                  _resident((1, Q_LORA)), _resident((Q_LORA, MLA_HEADS * MLA_QK_PAD)),
                  _resident((1, KV_LORA)), _resident((KV_LORA, MLA_HEADS * MLA_QK_PAD)),
                  _resident((1, MLA_QK_PAD)), _resident((1, MLA_QK_PAD)),
                  pos, pos, pos],
        out_specs=[head_blk(MLA_QK_PAD), head_blk(MLA_QK_PAD), head_blk(MLA_V)],
        out_shape=[jax.ShapeDtypeStruct((bsz, MLA_HEADS, seq, MLA_QK_PAD), BF16),
                   jax.ShapeDtypeStruct((bsz, MLA_HEADS, seq, MLA_QK_PAD), BF16),
                   jax.ShapeDtypeStruct((bsz, MLA_HEADS, seq, MLA_V), BF16)],
        compiler_params=_params("arbitrary", "arbitrary"),
        name="mla_prep",
    )(lat, g_cq, w_uq_p, g_ckv, w_ukv, g_qn_p, g_kn_p, cos_m, slo_m, shi_m)


def _attention_kernel(q_ref, k_ref, v_ref, o_ref, *, tk, nkv):
    q = q_ref[0, 0]
    tq = q.shape[0]

    def body(j, carry):
        m, l, acc = carry
        k0 = pl.multiple_of(j * tk, tk)
        kc = k_ref[0, 0, pl.ds(k0, tk), :]
        vc = v_ref[0, 0, pl.ds(k0, tk), :]
        s = _dot_nt(q, kc)
        m_new = jnp.maximum(m, jnp.max(s, axis=-1, keepdims=True))
        p = jnp.exp(s - m_new)
        alpha = jnp.exp(m - m_new)
        l = alpha * l + jnp.sum(p, axis=-1, keepdims=True)
        acc = alpha * acc + _dot(p.astype(BF16), vc)
        return m_new, l, acc

    init = (jnp.full((tq, 1), -jnp.inf, F32), jnp.zeros((tq, 1), F32), jnp.zeros((tq, MLA_V), F32))
    _, l, acc = lax.fori_loop(0, nkv, body, init)
    o_ref[0] = (acc / l).astype(BF16)


def _attention(q, k, v, tq, tk):
    bsz, nh, seq, _ = q.shape
    return pl.pallas_call(
        functools.partial(_attention_kernel, tk=tk, nkv=seq // tk),
        grid=(bsz, nh, seq // tq),
        in_specs=[pl.BlockSpec((1, 1, tq, MLA_QK_PAD), lambda b, h, i: (b, h, i, 0)),
                  pl.BlockSpec((1, 1, seq, MLA_QK_PAD), lambda b, h, i: (b, h, 0, 0)),
                  pl.BlockSpec((1, 1, seq, MLA_V), lambda b, h, i: (b, h, 0, 0))],
        out_specs=pl.BlockSpec((1, tq, MLA_V), lambda b, h, i: (b, i, h)),
        out_shape=jax.ShapeDtypeStruct((bsz, seq, nh * MLA_V), BF16),
        compiler_params=_params("arbitrary", "arbitrary", "arbitrary"),
        name="attention",
    )(q, k, v)


def _merge_kernel(x_ref, ret_ref, mla_ref, gates_ref, wr_ref, wm_ref, wo_ref, o_ref):
    ret_branch = _dot(ret_ref[...], wr_ref[...])
    mla_branch = _dot(mla_ref[...], wm_ref[...])
    merged = (gates_ref[:, :D_MODEL].astype(F32) * ret_branch
              + gates_ref[:, D_MODEL:].astype(F32) * mla_branch)
    o_ref[...] = x_ref[...] + _dot(merged.astype(BF16), wo_ref[...])


def _merge(x2, ret, mla, gates, w_ret_o, w_mla_o, w_out, tm):
    t = x2.shape[0]
    row = lambda w: pl.BlockSpec((tm, w), lambda i: (i, 0))
    sq = _resident((D_MODEL, D_MODEL))
    return pl.pallas_call(
        _merge_kernel,
        grid=(t // tm,),
        in_specs=[row(D_MODEL), row(RET_V_W), row(D_MODEL), row(2 * D_MODEL), sq, sq, sq],
        out_specs=row(D_MODEL),
        out_shape=jax.ShapeDtypeStruct((t, D_MODEL), F32),
        compiler_params=_params("arbitrary"),
        name="merge",
    )(x2, ret, mla, gates, w_ret_o, w_mla_o, w_out)


FFN_HALO = 8
FFN_NC = 256


def _ffn_kernel(x_ref, prev_ref, next_ref, g_ref, wup_ref, cw_ref, cb_ref, wdn_ref, o_ref, h_ref,
                *, ts, nblk):
    i = pl.program_id(1)
    g = g_ref[...]

    def rms(x):
        ms = jnp.mean(x * x, axis=-1, keepdims=True)
        return x * lax.rsqrt(ms + EPS) * g

    keep_prev = jnp.where(i > 0, 1.0, 0.0)
    keep_next = jnp.where(i < nblk - 1, 1.0, 0.0)
    xm = x_ref[0]
    h_ref[:FFN_HALO] = (rms(prev_ref[0]) * keep_prev).astype(BF16)
    h_ref[FFN_HALO:FFN_HALO + ts] = rms(xm).astype(BF16)
    h_ref[FFN_HALO + ts:] = (rms(next_ref[0]) * keep_next).astype(BF16)
    h = h_ref[...]

    def conv(u, c0):
        w = cw_ref[:, c0:c0 + FFN_NC]
        return (u[FFN_HALO - 1:FFN_HALO - 1 + ts] * w[0:1]
                + u[FFN_HALO:FFN_HALO + ts] * w[1:2]
                + u[FFN_HALO + 1:FFN_HALO + 1 + ts] * w[2:3]
                + cb_ref[:, c0:c0 + FFN_NC])

    acc = xm
    for n in range(D_FF // FFN_NC):
        ca = n * FFN_NC
        cg = D_FF + n * FFN_NC
        ua = conv(_dot(h, wup_ref[:, ca:ca + FFN_NC]), ca)
        ub = conv(_dot(h, wup_ref[:, cg:cg + FFN_NC]), cg)
        act = (ua * _sigmoid(ua) * ub).astype(BF16)
        acc = acc + _dot(act, wdn_ref[ca:ca + FFN_NC, :])
    o_ref[0] = acc


def _ffn(x1, g_ffn, w_up, conv_w, conv_b, w_down, ts):
    bsz, seq, _ = x1.shape
    nblk = seq // ts
    hb = ts // FFN_HALO
    last_halo = seq // FFN_HALO - 1
    return pl.pallas_call(
        functools.partial(_ffn_kernel, ts=ts, nblk=nblk),
        grid=(bsz, nblk),
        in_specs=[pl.BlockSpec((1, ts, D_MODEL), lambda b, i: (b, i, 0)),
                  pl.BlockSpec((1, FFN_HALO, D_MODEL), lambda b, i: (b, jnp.maximum(i * hb - 1, 0), 0)),
                  pl.BlockSpec((1, FFN_HALO, D_MODEL), lambda b, i: (b, jnp.minimum((i + 1) * hb, last_halo), 0)),
                  _resident((1, D_MODEL)), _resident((D_MODEL, 2 * D_FF)),
                  _resident((3, 2 * D_FF)), _resident((1, 2 * D_FF)), _resident((D_FF, D_MODEL))],
        out_specs=pl.BlockSpec((1, ts, D_MODEL), lambda b, i: (b, i, 0)),
        out_shape=jax.ShapeDtypeStruct((bsz, seq, D_MODEL), F32),
        scratch_shapes=[pltpu.VMEM((ts + 2 * FFN_HALO, D_MODEL), BF16)],
        compiler_params=_params("arbitrary", "arbitrary"),
        name="ffn",
    )(x1, x1, x1, g_ffn, w_up, conv_w, conv_b, w_down)


def _rope_tables(seq):
    pos = jnp.arange(seq).astype(F32)[:, None]

    def cos_sin(d):
        inv = ROPE_BASE ** (-jnp.arange(0, d, 2, dtype=F32) / d)
        ang = pos * inv[None, :]
        return jnp.cos(ang), jnp.sin(ang)

    cr, sr = cos_sin(RET_DK)
    cm, sm = cos_sin(MLA_ROPE)
    zh = jnp.zeros_like(sm)
    zpad = jnp.zeros((seq, LANES - MLA_ROPE), F32)
    ret = (jnp.concatenate([cr, cr], 1), jnp.concatenate([-sr, sr], 1))
    mla = (jnp.concatenate([cm, cm, zpad], 1), jnp.concatenate([-sm, zh, zpad], 1),
           jnp.concatenate([zh, sm, zpad], 1))
    return ret, mla


def _tile(seq, want):
    t = min(seq, want)
    assert seq % t == 0, (seq, t)
    return t


def _trunk(x, p, depth):
    bsz, seq, _ = x.shape
    t = bsz * seq
    (cos_r, sin_r), (cos_m, slo_m, shi_m) = _rope_tables(seq)
    tm = _tile(seq, 512)
    for l in range(depth):
        w = {k: v[l] for k, v in p.items()}
        x2 = x.reshape(t, D_MODEL)
        qk, rv, rg, lat, gates = _inproj(x2, w["g_mix"], w["w_in"], cos_r, sin_r, seq, tm)
        ret = _retention(w["dec"], w["ret_gn_g"],
                         qk.reshape(bsz, seq, -1), rv.reshape(bsz, seq, -1), rg.reshape(bsz, seq, -1),
                         _tile(seq, 512))
        q, k, v = _mla_prep(lat.reshape(bsz, seq, -1), w["g_cq"], w["w_uq"], w["g_ckv"], w["w_ukv"],
                            w["g_qn"], w["g_kn"], cos_m, slo_m, shi_m, _tile(seq, 512))
        mla = _attention(q, k, v, _tile(seq, 512), _tile(seq, 512))
        x1 = _merge(x2, ret.reshape(t, -1), mla.reshape(t, -1), gates,
                    w["w_ret_o"], w["w_mla_o"], w["w_out"], tm)
        x = _ffn(x1.reshape(bsz, seq, D_MODEL), w["g_ffn"], w["w_up"], w["conv_w"], w["conv_b"],
                 w["w_down"], _tile(seq, 512))
    return x


def _prepare_weights(g_mix, w_in, ret_decay_fwd, ret_decay_bwd, ret_gn_g, w_ret_o, g_cq, w_uq, g_ckv,
                     w_ukv, g_qn, g_kn, w_mla_o, w_out, g_ffn, w_up, conv_w, conv_b, w_down):
    depth = w_in.shape[0]
    kr_end = 2 * RET_Q_W + 2 * RET_V_W + Q_LORA + KV_LORA + MLA_ROPE
    w_in_p = jnp.concatenate(
        [w_in[:, :, :kr_end], jnp.zeros((depth, D_MODEL, LANES - MLA_ROPE), w_in.dtype), w_in[:, :, kr_end:]],
        axis=2).astype(BF16)
    pad_h = MLA_QK_PAD - MLA_QK
    w_uq_p = jnp.pad(w_uq.reshape(depth, Q_LORA, MLA_HEADS, MLA_QK), ((0, 0), (0, 0), (0, 0), (0, pad_h)))
    w_uq_p = w_uq_p.reshape(depth, Q_LORA, MLA_HEADS * MLA_QK_PAD).astype(BF16)
    dec = jnp.stack([ret_decay_fwd, ret_decay_bwd], axis=1).astype(F32)
    dec = jnp.broadcast_to(dec[..., None], (depth, 2, RET_HEADS, LANES))
    row = lambda a: a[:, None, :].astype(F32)
    return {
        "g_mix": row(g_mix), "w_in": w_in_p, "dec": dec, "ret_gn_g": row(ret_gn_g),
        "w_ret_o": w_ret_o.astype(BF16), "g_cq": row(g_cq), "w_uq": w_uq_p, "g_ckv": row(g_ckv),
        "w_ukv": w_ukv.astype(BF16),
        "g_qn": row(jnp.pad(g_qn, ((0, 0), (0, pad_h)))), "g_kn": row(jnp.pad(g_kn, ((0, 0), (0, pad_h)))),
        "w_mla_o": w_mla_o.astype(BF16), "w_out": w_out.astype(BF16), "g_ffn": row(g_ffn),
        "w_up": w_up.astype(BF16), "conv_w": conv_w.astype(F32), "conv_b": row(conv_b),
        "w_down": w_down.astype(BF16),
    }


def kernel(x_prompt, x_sample, g_mix, w_in, ret_decay_fwd, ret_decay_bwd, ret_gn_g, w_ret_o, g_cq, w_uq,
           g_ckv, w_ukv, g_qn, g_kn, w_mla_o, w_out, g_ffn, w_up, conv_w, conv_b, w_down):
    depth = w_in.shape[0]
    p = _prepare_weights(g_mix, w_in, ret_decay_fwd, ret_decay_bwd, ret_gn_g, w_ret_o, g_cq, w_uq, g_ckv,
                         w_ukv, g_qn, g_kn, w_mla_o, w_out, g_ffn, w_up, conv_w, conv_b, w_down)
    return (_trunk(x_prompt, p, depth), _trunk(x_sample, p, depth))
```

```python
import functools

import jax
import jax.numpy as jnp
from jax import lax
from jax.experimental import pallas as pl
from jax.experimental.pallas import tpu as pltpu

D_MODEL = 1024
RET_HEADS = 4
RET_DK = 128
RET_DV = 256
RET_CHUNK = 128
MLA_HEADS = 8
MLA_NOPE = 128
MLA_ROPE = 64
MLA_V = 128
MLA_QK = MLA_NOPE + MLA_ROPE
MLA_QK_PAD = 256
Q_LORA = 384
KV_LORA = 256
D_FF = 2816
ROPE_BASE = 10000.0
EPS = 1e-6
LOG2_E = 1.4426950408889634

RET_Q_W = RET_HEADS * RET_DK
RET_V_W = RET_HEADS * RET_DV
LATENT_W = Q_LORA + KV_LORA + 128
IN_W_PAD = 2 * RET_Q_W + 2 * RET_V_W + LATENT_W + 2 * D_MODEL

LANES = 128
VMEM_LIMIT_BYTES = 56 * 1024 * 1024

F32 = jnp.float32
BF16 = jnp.bfloat16


def _params(*semantics):
    return pltpu.CompilerParams(dimension_semantics=semantics, vmem_limit_bytes=VMEM_LIMIT_BYTES)


def _resident(shape):
    return pl.BlockSpec(shape, lambda *_: (0,) * len(shape), pipeline_mode=pl.Buffered(1))


def _sigmoid(x):
    return 1.0 / (1.0 + jnp.exp(-x))


def _dot(a, b):
    return jnp.dot(a, b, preferred_element_type=F32)


def _dot_nt(a, b):
    return lax.dot_general(a, b, (((1,), (1,)), ((), ())), preferred_element_type=F32)


def _dot_tn(a, b):
    return lax.dot_general(a, b, (((0,), (0,)), ((), ())), preferred_element_type=F32)


def _inproj_kernel(x_ref, g_ref, w_ref, cos_ref, sin_ref,
                   qk_ref, v_ref, rg_ref, lat_ref, gates_ref):
    xf = x_ref[...]
    ms = jnp.mean(xf * xf, axis=-1, keepdims=True)
    h = (xf * lax.rsqrt(ms + EPS) * g_ref[...]).astype(BF16)
    cos = cos_ref[...]
    sin = sin_ref[...]

    def proj(c0, width):
        return _dot(h, w_ref[:, c0:c0 + width])

    for seg, scale in ((0, None), (1, RET_DK ** -0.5)):
        acc = proj(seg * RET_Q_W, RET_Q_W)
        for hd in range(RET_HEADS):
            xs = acc[:, hd * RET_DK:(hd + 1) * RET_DK]
            r = xs * cos + pltpu.roll(xs, RET_DK // 2, 1) * sin
            if scale is not None:
                r = r * scale
            c0 = seg * RET_Q_W + hd * RET_DK
            qk_ref[:, c0:c0 + RET_DK] = r.astype(BF16)
    base = 2 * RET_Q_W
    half = RET_V_W // 2
    for j in range(2):
        v_ref[:, j * half:(j + 1) * half] = proj(base + j * half, half).astype(BF16)
    base += RET_V_W
    for j in range(2):
        a = proj(base + j * half, half)
        rg_ref[:, j * half:(j + 1) * half] = (a * _sigmoid(a)).astype(BF16)
    base += RET_V_W
    lat_ref[...] = proj(base, LATENT_W).astype(BF16)
    base += LATENT_W
    for j in range(4):
        a = proj(base + j * half, half)
        gates_ref[:, j * half:(j + 1) * half] = _sigmoid(a).astype(BF16)


def _inproj(x2, g_mix, w_in_p, cos_r, sin_r, seq, tm):
    t = x2.shape[0]
    nseq = seq // tm
    row = lambda w: pl.BlockSpec((tm, w), lambda i: (i, 0))
    pos = pl.BlockSpec((tm, LANES), lambda i: (i % nseq, 0))
    out_w = (2 * RET_Q_W, RET_V_W, RET_V_W, LATENT_W, 2 * D_MODEL)
    return pl.pallas_call(
        _inproj_kernel,
        grid=(t // tm,),
        in_specs=[row(D_MODEL), _resident((1, D_MODEL)), _resident((D_MODEL, IN_W_PAD)), pos, pos],
        out_specs=[row(w) for w in out_w],
        out_shape=[jax.ShapeDtypeStruct((t, w), BF16) for w in out_w],
        compiler_params=_params("arbitrary"),
        name="inproj",
    )(x2, g_mix, w_in_p, cos_r, sin_r)


_T_DMAT, _T_QF, _T_QB, _T_KF, _T_KB, _T_CF, _T_CB = range(7)


def _retention_kernel(dec_ref, gn_ref, qk_ref, v_ref, rg_ref, o_ref,
                      tab_ref, sf_ref, sb_ref, sball_ref, *, ts, nblk):
    c = RET_CHUNK
    ncb = ts // c
    b = pl.program_id(0)
    ph = pl.program_id(1)
    i = pl.program_id(2)

    @pl.when((b == 0) & (ph == 0) & (i == 0))
    def _tables():
        row = lax.broadcasted_iota(jnp.int32, (c, c), 0).astype(F32)
        col = lax.broadcasted_iota(jnp.int32, (c, c), 1).astype(F32)
        diff = row - col
        for hd in range(RET_HEADS):
            def log_sigmoid(d):
                return jnp.minimum(d, 0.0) - jnp.log1p(jnp.exp(-jnp.abs(d)))
            lgf = log_sigmoid(dec_ref[0, hd:hd + 1, :])
            lgb = log_sigmoid(dec_ref[1, hd:hd + 1, :])
            tab_ref[hd, _T_DMAT] = jnp.where(diff >= 0.0,
                                             jnp.exp(lgf * jnp.maximum(diff, 0.0)),
                                             jnp.exp(lgb * jnp.maximum(-diff, 0.0)))
            tab_ref[hd, _T_QF] = jnp.exp(lgf * (row + 1.0))
            tab_ref[hd, _T_QB] = jnp.exp(lgb * (c - row))
            tab_ref[hd, _T_KF] = jnp.exp(lgf * (c - 1.0 - row))
            tab_ref[hd, _T_KB] = jnp.exp(lgb * row)
            tab_ref[hd, _T_CF] = jnp.exp(lgf * (row * 0.0 + c))
            tab_ref[hd, _T_CB] = jnp.exp(lgb * (row * 0.0 + c))

    def chunk_decay(hd, which):
        t = tab_ref[hd, which]
        return jnp.concatenate([t, t], axis=1)

    @pl.when(ph == 0)
    def _backward_states():
        @pl.when(i == 0)
        def _():
            sb_ref[...] = jnp.zeros_like(sb_ref)
        blk = nblk - 1 - i
        for cb in reversed(range(ncb)):
            r0 = cb * c
            for hd in range(RET_HEADS):
                k = qk_ref[0, r0:r0 + c, RET_Q_W + hd * RET_DK:RET_Q_W + (hd + 1) * RET_DK]
                v = v_ref[0, r0:r0 + c, hd * RET_DV:(hd + 1) * RET_DV]
                st = sb_ref[hd]
                sball_ref[blk * ncb + cb, hd] = st.astype(BF16)
                kd = (k.astype(F32) * tab_ref[hd, _T_KB]).astype(BF16)
                sb_ref[hd] = st * chunk_decay(hd, _T_CB) + _dot_tn(kd, v)

    @pl.when(ph == 1)
    def _forward():
        @pl.when(i == 0)
        def _():
            sf_ref[...] = jnp.zeros_like(sf_ref)
        for cb in range(ncb):
            r0 = cb * c
            for hd in range(RET_HEADS):
                q = qk_ref[0, r0:r0 + c, hd * RET_DK:(hd + 1) * RET_DK]
                k = qk_ref[0, r0:r0 + c, RET_Q_W + hd * RET_DK:RET_Q_W + (hd + 1) * RET_DK]
                v = v_ref[0, r0:r0 + c, hd * RET_DV:(hd + 1) * RET_DV]
                qf32 = q.astype(F32)
                kf32 = k.astype(F32)
                sd = (_dot_nt(q, k) * tab_ref[hd, _T_DMAT]).astype(BF16)
                st = sf_ref[hd]
                out = _dot(sd, v)
                out = out + _dot((qf32 * tab_ref[hd, _T_QF]).astype(BF16), st.astype(BF16))
                out = out + _dot((qf32 * tab_ref[hd, _T_QB]).astype(BF16), sball_ref[i * ncb + cb, hd])
                kd = (kf32 * tab_ref[hd, _T_KF]).astype(BF16)
                sf_ref[hd] = st * chunk_decay(hd, _T_CF) + _dot_tn(kd, v)
                mu = jnp.mean(out, axis=-1, keepdims=True)
                xc = out - mu
                var = jnp.mean(xc * xc, axis=-1, keepdims=True)
                y = xc * lax.rsqrt(var + EPS) * gn_ref[:, hd * RET_DV:(hd + 1) * RET_DV]
                gate = rg_ref[0, r0:r0 + c, hd * RET_DV:(hd + 1) * RET_DV].astype(F32)
                o_ref[0, r0:r0 + c, hd * RET_DV:(hd + 1) * RET_DV] = (y * gate).astype(BF16)


def _retention(dec, gn_g, qk, v, rg, ts):
    bsz, seq, _ = qk.shape
    nblk = seq // ts
    nchunk = seq // RET_CHUNK
    sweep = lambda b, ph, i: (b, i * ph + (nblk - 1 - i) * (1 - ph), 0)
    fwd_only = lambda b, ph, i: (b, i * ph, 0)
    return pl.pallas_call(
        functools.partial(_retention_kernel, ts=ts, nblk=nblk),
        grid=(bsz, 2, nblk),
        in_specs=[_resident((2, RET_HEADS, LANES)), _resident((1, RET_V_W)),
                  pl.BlockSpec((1, ts, 2 * RET_Q_W), sweep),
                  pl.BlockSpec((1, ts, RET_V_W), sweep),
                  pl.BlockSpec((1, ts, RET_V_W), fwd_only)],
        out_specs=pl.BlockSpec((1, ts, RET_V_W), fwd_only),
        out_shape=jax.ShapeDtypeStruct((bsz, seq, RET_V_W), BF16),
        scratch_shapes=[pltpu.VMEM((RET_HEADS, 7, RET_CHUNK, RET_CHUNK), F32),
                        pltpu.VMEM((RET_HEADS, RET_DK, RET_DV), F32),
                        pltpu.VMEM((RET_HEADS, RET_DK, RET_DV), F32),
                        pltpu.VMEM((nchunk, RET_HEADS, RET_DK, RET_DV), BF16)],
        compiler_params=_params("arbitrary", "arbitrary", "arbitrary"),
        name="retention",
    )(dec, gn_g, qk, v, rg)


def _rope64(x, cos, sin_lo, sin_hi):
    return x * cos + pltpu.roll(x, LANES - MLA_ROPE // 2, 1) * sin_lo + pltpu.roll(x, MLA_ROPE // 2, 1) * sin_hi


def _mla_prep_kernel(lat_ref, gcq_ref, wuq_ref, gckv_ref, wukv_ref, gqn_ref, gkn_ref,
                     cos_ref, slo_ref, shi_ref, qt_ref, k_ref, vt_ref):
    ts = lat_ref.shape[1]
    cos = cos_ref[...]
    slo = slo_ref[...]
    shi = shi_ref[...]

    def rms(x, g):
        ms = jnp.mean(x * x, axis=-1, keepdims=True)
        return x * lax.rsqrt(ms + EPS) * g

    cq = lat_ref[0, :, :Q_LORA].astype(F32)
    ckv = lat_ref[0, :, Q_LORA:Q_LORA + KV_LORA].astype(F32)
    kr = lat_ref[0, :, Q_LORA + KV_LORA:].astype(F32)
    cqn = rms(cq, gcq_ref[...]).astype(BF16)
    ckvn = rms(ckv, gckv_ref[...]).astype(BF16)
    gqn = gqn_ref[...]
    gkn = gkn_ref[...]
    kr = _rope64(kr, cos, slo, shi)
    kr_sq = jnp.sum(kr * kr, axis=-1, keepdims=True)
    kr_rot = _rope64(kr * gkn[:, MLA_NOPE:], cos, slo, shi)
    qscale = LOG2_E * MLA_QK ** -0.5
    for hd in range(MLA_HEADS):
        c0 = hd * MLA_QK_PAD
        qh = _dot(cqn, wuq_ref[:, c0:c0 + MLA_QK_PAD])
        ssq = jnp.sum(qh * qh, axis=-1, keepdims=True) * (1.0 / MLA_QK)
        qn = qh * lax.rsqrt(ssq + EPS) * gqn
        qh = jnp.concatenate([qn[:, :MLA_NOPE], _rope64(qn[:, MLA_NOPE:], cos, slo, shi)], axis=1) * qscale
        qt_ref[0, hd] = qh.T.astype(BF16)
        kvh = _dot(ckvn, wukv_ref[:, c0:c0 + MLA_QK_PAD])
        kn = kvh[:, :MLA_NOPE]
        ssk = (jnp.sum(kn * kn, axis=-1, keepdims=True) + kr_sq) * (1.0 / MLA_QK)
        rk = lax.rsqrt(ssk + EPS)
        k_ref[0, hd, :, :MLA_NOPE] = (kn * rk * gkn[:, :MLA_NOPE]).astype(BF16)
        k_ref[0, hd, :, MLA_NOPE:] = (kr_rot * rk).astype(BF16)
        for c in range(ts // LANES):
            vt_ref[0, hd, c] = kvh[c * LANES:(c + 1) * LANES, MLA_NOPE:].T.astype(BF16)


def _mla_prep(lat, g_cq, w_uq_p, g_ckv, w_ukv, g_qn_p, g_kn_p, cos_m, slo_m, shi_m, ts):
    bsz, seq, _ = lat.shape
    pos = pl.BlockSpec((ts, LANES), lambda b, i: (i, 0))
    head_blk = lambda w: pl.BlockSpec((1, MLA_HEADS, ts, w), lambda b, i: (b, 0, i, 0))
    return pl.pallas_call(
        _mla_prep_kernel,
        grid=(bsz, seq // ts),
        in_specs=[pl.BlockSpec((1, ts, LATENT_W), lambda b, i: (b, i, 0)),
                  _resident((1, Q_LORA)), _resident((Q_LORA, MLA_HEADS * MLA_QK_PAD)),
                  _resident((1, KV_LORA)), _resident((KV_LORA, MLA_HEADS * MLA_QK_PAD)),
                  _resident((1, MLA_QK_PAD)), _resident((1, MLA_QK_PAD)),
                  pos, pos, pos],
        out_specs=[pl.BlockSpec((1, MLA_HEADS, MLA_QK_PAD, ts), lambda b, i: (b, 0, 0, i)),
                   head_blk(MLA_QK_PAD),
                   pl.BlockSpec((1, MLA_HEADS, ts // LANES, MLA_V, LANES), lambda b, i: (b, 0, i, 0, 0))],
        out_shape=[jax.ShapeDtypeStruct((bsz, MLA_HEADS, MLA_QK_PAD, seq), BF16),
                   jax.ShapeDtypeStruct((bsz, MLA_HEADS, seq, MLA_QK_PAD), BF16),
                   jax.ShapeDtypeStruct((bsz, MLA_HEADS, seq // LANES, MLA_V, LANES), BF16)],
        compiler_params=_params("arbitrary", "arbitrary"),
        name="mla_prep",
    )(lat, g_cq, w_uq_p, g_ckv, w_ukv, g_qn_p, g_kn_p, cos_m, slo_m, shi_m)


def _attention_kernel(qt_ref, k_ref, vt_ref, o_ref, s0_ref, s1_ref, *, tk, nkv):
    qt = qt_ref[0, 0]
    tq = qt.shape[1]
    sub = tk // LANES

    def scores(j):
        k0 = j * tk
        if not isinstance(j, int):
            k0 = pl.multiple_of(k0, tk)
        return _dot(k_ref[0, 0, pl.ds(k0, tk), :], qt)

    def update(s_ref, j, carry):
        m, l, acc = carry
        s = s_ref[...]
        m_new = jnp.maximum(m, jnp.max(s, axis=0, keepdims=True))
        p = jnp.exp2(s - m_new)
        alpha = jnp.exp2(m - m_new)
        l = alpha * l + jnp.sum(p, axis=0, keepdims=True)
        vt3 = vt_ref[0, 0, pl.ds(j * sub, sub)]
        vt = jnp.concatenate([vt3[i] for i in range(sub)], axis=1)
        acc = alpha * acc + _dot(vt, p.astype(BF16))
        return m_new, l, acc

    def pair(jj, carry):
        j = 2 * jj
        s1_ref[...] = scores(j + 1)
        carry = update(s0_ref, j, carry)
        s0_ref[...] = scores(j + 2)
        return update(s1_ref, j + 1, carry)

    carry = (jnp.full((1, tq), -jnp.inf, F32), jnp.zeros((1, tq), F32), jnp.zeros((MLA_V, tq), F32))
    s0_ref[...] = scores(0)
    carry = lax.fori_loop(0, nkv // 2 - 1, pair, carry)
    s1_ref[...] = scores(nkv - 1)
    carry = update(s0_ref, nkv - 2, carry)
    _, l, acc = update(s1_ref, nkv - 1, carry)
    o_ref[0] = (acc / l).T.astype(BF16)


def _attention(qt, k, vt, tq, tk):
    bsz, nh, seq, _ = k.shape
    nkv = seq // tk
    assert nkv >= 2 and nkv % 2 == 0 and tk % LANES == 0, (seq, tk)
    return pl.pallas_call(
        functools.partial(_attention_kernel, tk=tk, nkv=nkv),
        grid=(bsz, nh, seq // tq),
        in_specs=[pl.BlockSpec((1, 1, MLA_QK_PAD, tq), lambda b, h, i: (b, h, 0, i)),
                  pl.BlockSpec((1, 1, seq, MLA_QK_PAD), lambda b, h, i: (b, h, 0, 0)),
                  pl.BlockSpec((1, 1, seq // LANES, MLA_V, LANES), lambda b, h, i: (b, h, 0, 0, 0))],
        out_specs=pl.BlockSpec((1, tq, MLA_V), lambda b, h, i: (b, i, h)),
        out_shape=jax.ShapeDtypeStruct((bsz, seq, nh * MLA_V), BF16),
        scratch_shapes=[pltpu.VMEM((tk, tq), F32), pltpu.VMEM((tk, tq), F32)],
        compiler_params=_params("arbitrary", "arbitrary", "arbitrary"),
        name="attention",
    )(qt, k, vt)


def _merge_kernel(x_ref, ret_ref, mla_ref, gates_ref, wr_ref, wm_ref, wo_ref, o_ref):
    ret_branch = _dot(ret_ref[...], wr_ref[...])
    mla_branch = _dot(mla_ref[...], wm_ref[...])
    merged = (gates_ref[:, :D_MODEL].astype(F32) * ret_branch
              + gates_ref[:, D_MODEL:].astype(F32) * mla_branch)
    o_ref[...] = x_ref[...] + _dot(merged.astype(BF16), wo_ref[...])


def _merge(x2, ret, mla, gates, w_ret_o, w_mla_o, w_out, tm):
    t = x2.shape[0]
    row = lambda w: pl.BlockSpec((tm, w), lambda i: (i, 0))
    sq = _resident((D_MODEL, D_MODEL))
    return pl.pallas_call(
        _merge_kernel,
        grid=(t // tm,),
        in_specs=[row(D_MODEL), row(RET_V_W), row(D_MODEL), row(2 * D_MODEL), sq, sq, sq],
        out_specs=row(D_MODEL),
        out_shape=jax.ShapeDtypeStruct((t, D_MODEL), F32),
        compiler_params=_params("arbitrary"),
        name="merge",
    )(x2, ret, mla, gates, w_ret_o, w_mla_o, w_out)


FFN_HALO = 8
FFN_NC = 256


def _ffn_kernel(x_ref, prev_ref, next_ref, g_ref, wup_ref, cw_ref, cb_ref, wdn_ref, o_ref, h_ref,
                *, ts, nblk):
    i = pl.program_id(1)
    g = g_ref[...]

    def rms(x):
        ms = jnp.mean(x * x, axis=-1, keepdims=True)
        return x * lax.rsqrt(ms + EPS) * g

    keep_prev = jnp.where(i > 0, 1.0, 0.0)
    keep_next = jnp.where(i < nblk - 1, 1.0, 0.0)
    xm = x_ref[0]
    h_ref[:FFN_HALO] = (rms(prev_ref[0]) * keep_prev).astype(BF16)
    h_ref[FFN_HALO:FFN_HALO + ts] = rms(xm).astype(BF16)
    h_ref[FFN_HALO + ts:] = (rms(next_ref[0]) * keep_next).astype(BF16)
    h = h_ref[...]

    def conv(u, c0):
        w = cw_ref[:, c0:c0 + FFN_NC]
        return (u[FFN_HALO - 1:FFN_HALO - 1 + ts] * w[0:1]
                + u[FFN_HALO:FFN_HALO + ts] * w[1:2]
                + u[FFN_HALO + 1:FFN_HALO + 1 + ts] * w[2:3]
                + cb_ref[:, c0:c0 + FFN_NC])

    acc = xm
    for n in range(D_FF // FFN_NC):
        ca = n * FFN_NC
        cg = D_FF + n * FFN_NC
        ua = conv(_dot(h, wup_ref[:, ca:ca + FFN_NC]), ca)
        ub = conv(_dot(h, wup_ref[:, cg:cg + FFN_NC]), cg)
        act = (ua * _sigmoid(ua) * ub).astype(BF16)
        acc = acc + _dot(act, wdn_ref[ca:ca + FFN_NC, :])
    o_ref[0] = acc


def _ffn(x1, g_ffn, w_up, conv_w, conv_b, w_down, ts):
    bsz, seq, _ = x1.shape
    nblk = seq // ts
    hb = ts // FFN_HALO
    last_halo = seq // FFN_HALO - 1
    return pl.pallas_call(
        functools.partial(_ffn_kernel, ts=ts, nblk=nblk),
        grid=(bsz, nblk),
        in_specs=[pl.BlockSpec((1, ts, D_MODEL), lambda b, i: (b, i, 0)),
                  pl.BlockSpec((1, FFN_HALO, D_MODEL), lambda b, i: (b, jnp.maximum(i * hb - 1, 0), 0)),
                  pl.BlockSpec((1, FFN_HALO, D_MODEL), lambda b, i: (b, jnp.minimum((i + 1) * hb, last_halo), 0)),
                  _resident((1, D_MODEL)), _resident((D_MODEL, 2 * D_FF)),
                  _resident((3, 2 * D_FF)), _resident((1, 2 * D_FF)), _resident((D_FF, D_MODEL))],
        out_specs=pl.BlockSpec((1, ts, D_MODEL), lambda b, i: (b, i, 0)),
        out_shape=jax.ShapeDtypeStruct((bsz, seq, D_MODEL), F32),
        scratch_shapes=[pltpu.VMEM((ts + 2 * FFN_HALO, D_MODEL), BF16)],
        compiler_params=_params("arbitrary", "arbitrary"),
        name="ffn",
    )(x1, x1, x1, g_ffn, w_up, conv_w, conv_b, w_down)


def _rope_tables(seq):
    pos = jnp.arange(seq).astype(F32)[:, None]

    def cos_sin(d):
        inv = ROPE_BASE ** (-jnp.arange(0, d, 2, dtype=F32) / d)
        ang = pos * inv[None, :]
        return jnp.cos(ang), jnp.sin(ang)

    cr, sr = cos_sin(RET_DK)
    cm, sm = cos_sin(MLA_ROPE)
    zh = jnp.zeros_like(sm)
    zpad = jnp.zeros((seq, LANES - MLA_ROPE), F32)
    ret = (jnp.concatenate([cr, cr], 1), jnp.concatenate([-sr, sr], 1))
    mla = (jnp.concatenate([cm, cm, zpad], 1), jnp.concatenate([-sm, zh, zpad], 1),
           jnp.concatenate([zh, sm, zpad], 1))
    return ret, mla


def _tile(seq, want):
    t = min(seq, want)
    assert seq % t == 0, (seq, t)
    return t


def _trunk(x, p, depth):
    bsz, seq, _ = x.shape
    t = bsz * seq
    (cos_r, sin_r), (cos_m, slo_m, shi_m) = _rope_tables(seq)
    tm = _tile(seq, 512)
    for l in range(depth):
        w = {k: v[l] for k, v in p.items()}
        x2 = x.reshape(t, D_MODEL)
        qk, rv, rg, lat, gates = _inproj(x2, w["g_mix"], w["w_in"], cos_r, sin_r, seq, tm)
        ret = _retention(w["dec"], w["ret_gn_g"],
                         qk.reshape(bsz, seq, -1), rv.reshape(bsz, seq, -1), rg.reshape(bsz, seq, -1),
                         _tile(seq, 512))
        q, k, v = _mla_prep(lat.reshape(bsz, seq, -1), w["g_cq"], w["w_uq"], w["g_ckv"], w["w_ukv"],
                            w["g_qn"], w["g_kn"], cos_m, slo_m, shi_m, _tile(seq, 512))
        mla = _attention(q, k, v, _tile(seq, 512), _tile(seq, 512))
        x1 = _merge(x2, ret.reshape(t, -1), mla.reshape(t, -1), gates,
                    w["w_ret_o"], w["w_mla_o"], w["w_out"], tm)
        x = _ffn(x1.reshape(bsz, seq, D_MODEL), w["g_ffn"], w["w_up"], w["conv_w"], w["conv_b"],
                 w["w_down"], _tile(seq, 512))
    return x


def _prepare_weights(g_mix, w_in, ret_decay_fwd, ret_decay_bwd, ret_gn_g, w_ret_o, g_cq, w_uq, g_ckv,
                     w_ukv, g_qn, g_kn, w_mla_o, w_out, g_ffn, w_up, conv_w, conv_b, w_down):
    depth = w_in.shape[0]
    kr_end = 2 * RET_Q_W + 2 * RET_V_W + Q_LORA + KV_LORA + MLA_ROPE
    w_in_p = jnp.concatenate(
        [w_in[:, :, :kr_end], jnp.zeros((depth, D_MODEL, LANES - MLA_ROPE), w_in.dtype), w_in[:, :, kr_end:]],
        axis=2).astype(BF16)
    pad_h = MLA_QK_PAD - MLA_QK
    w_uq_p = jnp.pad(w_uq.reshape(depth, Q_LORA, MLA_HEADS, MLA_QK), ((0, 0), (0, 0), (0, 0), (0, pad_h)))
    w_uq_p = w_uq_p.reshape(depth, Q_LORA, MLA_HEADS * MLA_QK_PAD).astype(BF16)
    dec = jnp.stack([ret_decay_fwd, ret_decay_bwd], axis=1).astype(F32)
    dec = jnp.broadcast_to(dec[..., None], (depth, 2, RET_HEADS, LANES))
    row = lambda a: a[:, None, :].astype(F32)
    return {
        "g_mix": row(g_mix), "w_in": w_in_p, "dec": dec, "ret_gn_g": row(ret_gn_g),
        "w_ret_o": w_ret_o.astype(BF16), "g_cq": row(g_cq), "w_uq": w_uq_p, "g_ckv": row(g_ckv),
        "w_ukv": w_ukv.astype(BF16),
        "g_qn": row(jnp.pad(g_qn, ((0, 0), (0, pad_h)))), "g_kn": row(jnp.pad(g_kn, ((0, 0), (0, pad_h)))),
        "w_mla_o": w_mla_o.astype(BF16), "w_out": w_out.astype(BF16), "g_ffn": row(g_ffn),
        "w_up": w_up.astype(BF16), "conv_w": conv_w.astype(F32), "conv_b": row(conv_b),
        "w_down": w_down.astype(BF16),
    }


def kernel(x_prompt, x_sample, g_mix, w_in, ret_decay_fwd, ret_decay_bwd, ret_gn_g, w_ret_o, g_cq, w_uq,
           g_ckv, w_ukv, g_qn, g_kn, w_mla_o, w_out, g_ffn, w_up, conv_w, conv_b, w_down):
    depth = w_in.shape[0]
    p = _prepare_weights(g_mix, w_in, ret_decay_fwd, ret_decay_bwd, ret_gn_g, w_ret_o, g_cq, w_uq, g_ckv,
                         w_ukv, g_qn, g_kn, w_mla_o, w_out, g_ffn, w_up, conv_w, conv_b, w_down)
    return (_trunk(x_prompt, p, depth), _trunk(x_sample, p, depth))
```

```python
import functools

import jax
import jax.numpy as jnp
from jax import lax
from jax.experimental import pallas as pl
from jax.experimental.pallas import tpu as pltpu

D_MODEL = 1024
RET_HEADS = 4
RET_DK = 128
RET_DV = 256
RET_CHUNK = 128
MLA_HEADS = 8
MLA_NOPE = 128
MLA_ROPE = 64
MLA_V = 128
MLA_QK = MLA_NOPE + MLA_ROPE
MLA_QK_PAD = 256
Q_LORA = 384
KV_LORA = 256
D_FF = 2816
ROPE_BASE = 10000.0
EPS = 1e-6
LOG2_E = 1.4426950408889634

RET_Q_W = RET_HEADS * RET_DK
RET_V_W = RET_HEADS * RET_DV
LATENT_W = Q_LORA + KV_LORA + 128
IN_W_PAD = 2 * RET_Q_W + 2 * RET_V_W + LATENT_W + 2 * D_MODEL

LANES = 128
VMEM_LIMIT_BYTES = 56 * 1024 * 1024

F32 = jnp.float32
BF16 = jnp.bfloat16


def _params(*semantics):
    return pltpu.CompilerParams(dimension_semantics=semantics, vmem_limit_bytes=VMEM_LIMIT_BYTES)


def _resident(shape):
    return pl.BlockSpec(shape, lambda *_: (0,) * len(shape), pipeline_mode=pl.Buffered(1))


def _sigmoid(x):
    return 1.0 / (1.0 + jnp.exp(-x))


def _dot(a, b):
    return jnp.dot(a, b, preferred_element_type=F32)


def _dot_nt(a, b):
    return lax.dot_general(a, b, (((1,), (1,)), ((), ())), preferred_element_type=F32)


def _dot_tn(a, b):
    return lax.dot_general(a, b, (((0,), (0,)), ((), ())), preferred_element_type=F32)


def _inproj_kernel(x_ref, g_ref, w_ref, cos_ref, sin_ref,
                   qk_ref, v_ref, rg_ref, lat_ref, gates_ref):
    xf = x_ref[...]
    ms = jnp.mean(xf * xf, axis=-1, keepdims=True)
    h = (xf * lax.rsqrt(ms + EPS) * g_ref[...]).astype(BF16)
    cos = cos_ref[...]
    sin = sin_ref[...]

    def proj(c0, width):
        return _dot(h, w_ref[:, c0:c0 + width])

    for seg, scale in ((0, None), (1, RET_DK ** -0.5)):
        acc = proj(seg * RET_Q_W, RET_Q_W)
        for hd in range(RET_HEADS):
            xs = acc[:, hd * RET_DK:(hd + 1) * RET_DK]
            r = xs * cos + pltpu.roll(xs, RET_DK // 2, 1) * sin
            if scale is not None:
                r = r * scale
            c0 = seg * RET_Q_W + hd * RET_DK
            qk_ref[:, c0:c0 + RET_DK] = r.astype(BF16)
    base = 2 * RET_Q_W
    half = RET_V_W // 2
    for j in range(2):
        v_ref[:, j * half:(j + 1) * half] = proj(base + j * half, half).astype(BF16)
    base += RET_V_W
    for j in range(2):
        a = proj(base + j * half, half)
        rg_ref[:, j * half:(j + 1) * half] = (a * _sigmoid(a)).astype(BF16)
    base += RET_V_W
    lat_ref[...] = proj(base, LATENT_W).astype(BF16)
    base += LATENT_W
    for j in range(4):
        a = proj(base + j * half, half)
        gates_ref[:, j * half:(j + 1) * half] = _sigmoid(a).astype(BF16)


def _inproj(x2, g_mix, w_in_p, cos_r, sin_r, seq, tm):
    t = x2.shape[0]
    nseq = seq // tm
    row = lambda w: pl.BlockSpec((tm, w), lambda i: (i, 0))
    pos = pl.BlockSpec((tm, LANES), lambda i: (i % nseq, 0))
    out_w = (2 * RET_Q_W, RET_V_W, RET_V_W, LATENT_W, 2 * D_MODEL)
    return pl.pallas_call(
        _inproj_kernel,
        grid=(t // tm,),
        in_specs=[row(D_MODEL), _resident((1, D_MODEL)), _resident((D_MODEL, IN_W_PAD)), pos, pos],
        out_specs=[row(w) for w in out_w],
        out_shape=[jax.ShapeDtypeStruct((t, w), BF16) for w in out_w],
        compiler_params=_params("arbitrary"),
        name="inproj",
    )(x2, g_mix, w_in_p, cos_r, sin_r)


_T_DMAT, _T_QF, _T_QB, _T_KF, _T_KB, _T_CF, _T_CB = range(7)


def _retention_kernel(dec_ref, gn_ref, qk_ref, v_ref, rg_ref, o_ref,
                      tab_ref, sf_ref, sb_ref, sball_ref, *, ts, nblk):
    c = RET_CHUNK
    ncb = ts // c
    b = pl.program_id(0)
    ph = pl.program_id(1)
    i = pl.program_id(2)

    @pl.when((b == 0) & (ph == 0) & (i == 0))
    def _tables():
        row = lax.broadcasted_iota(jnp.int32, (c, c), 0).astype(F32)
        col = lax.broadcasted_iota(jnp.int32, (c, c), 1).astype(F32)
        diff = row - col
        for hd in range(RET_HEADS):
            def log_sigmoid(d):
                return jnp.minimum(d, 0.0) - jnp.log1p(jnp.exp(-jnp.abs(d)))
            lgf = log_sigmoid(dec_ref[0, hd:hd + 1, :])
            lgb = log_sigmoid(dec_ref[1, hd:hd + 1, :])
            tab_ref[hd, _T_DMAT] = jnp.where(diff >= 0.0,
                                             jnp.exp(lgf * jnp.maximum(diff, 0.0)),
                                             jnp.exp(lgb * jnp.maximum(-diff, 0.0)))
            tab_ref[hd, _T_QF] = jnp.exp(lgf * (row + 1.0))
            tab_ref[hd, _T_QB] = jnp.exp(lgb * (c - row))
            tab_ref[hd, _T_KF] = jnp.exp(lgf * (c - 1.0 - row))
            tab_ref[hd, _T_KB] = jnp.exp(lgb * row)
            tab_ref[hd, _T_CF] = jnp.exp(lgf * (row * 0.0 + c))
            tab_ref[hd, _T_CB] = jnp.exp(lgb * (row * 0.0 + c))

    def chunk_decay(hd, which):
        t = tab_ref[hd, which]
        return jnp.concatenate([t, t], axis=1)

    @pl.when(ph == 0)
    def _backward_states():
        @pl.when(i == 0)
        def _():
            sb_ref[...] = jnp.zeros_like(sb_ref)
        blk = nblk - 1 - i
        for cb in reversed(range(ncb)):
            r0 = cb * c
            for hd in range(RET_HEADS):
                k = qk_ref[0, r0:r0 + c, RET_Q_W + hd * RET_DK:RET_Q_W + (hd + 1) * RET_DK]
                v = v_ref[0, r0:r0 + c, hd * RET_DV:(hd + 1) * RET_DV]
                st = sb_ref[hd]
                sball_ref[blk * ncb + cb, hd] = st.astype(BF16)
                kd = (k.astype(F32) * tab_ref[hd, _T_KB]).astype(BF16)
                sb_ref[hd] = st * chunk_decay(hd, _T_CB) + _dot_tn(kd, v)

    @pl.when(ph == 1)
    def _forward():
        @pl.when(i == 0)
        def _():
            sf_ref[...] = jnp.zeros_like(sf_ref)
        for cb in range(ncb):
            r0 = cb * c
            for hd in range(RET_HEADS):
                q = qk_ref[0, r0:r0 + c, hd * RET_DK:(hd + 1) * RET_DK]
                k = qk_ref[0, r0:r0 + c, RET_Q_W + hd * RET_DK:RET_Q_W + (hd + 1) * RET_DK]
                v = v_ref[0, r0:r0 + c, hd * RET_DV:(hd + 1) * RET_DV]
                qf32 = q.astype(F32)
                kf32 = k.astype(F32)
                sd = (_dot_nt(q, k) * tab_ref[hd, _T_DMAT]).astype(BF16)
                st = sf_ref[hd]
                out = _dot(sd, v)
                out = out + _dot((qf32 * tab_ref[hd, _T_QF]).astype(BF16), st.astype(BF16))
                out = out + _dot((qf32 * tab_ref[hd, _T_QB]).astype(BF16), sball_ref[i * ncb + cb, hd])
                kd = (kf32 * tab_ref[hd, _T_KF]).astype(BF16)
                sf_ref[hd] = st * chunk_decay(hd, _T_CF) + _dot_tn(kd, v)
                mu = jnp.mean(out, axis=-1, keepdims=True)
                xc = out - mu
                var = jnp.mean(xc * xc, axis=-1, keepdims=True)
                y = xc * lax.rsqrt(var + EPS) * gn_ref[:, hd * RET_DV:(hd + 1) * RET_DV]
                gate = rg_ref[0, r0:r0 + c, hd * RET_DV:(hd + 1) * RET_DV].astype(F32)
                o_ref[0, r0:r0 + c, hd * RET_DV:(hd + 1) * RET_DV] = (y * gate).astype(BF16)


def _retention(dec, gn_g, qk, v, rg, ts):
    bsz, seq, _ = qk.shape
    nblk = seq // ts
    nchunk = seq // RET_CHUNK
    sweep = lambda b, ph, i: (b, i * ph + (nblk - 1 - i) * (1 - ph), 0)
    fwd_only = lambda b, ph, i: (b, i * ph, 0)
    return pl.pallas_call(
        functools.partial(_retention_kernel, ts=ts, nblk=nblk),
        grid=(bsz, 2, nblk),
        in_specs=[_resident((2, RET_HEADS, LANES)), _resident((1, RET_V_W)),
                  pl.BlockSpec((1, ts, 2 * RET_Q_W), sweep),
                  pl.BlockSpec((1, ts, RET_V_W), sweep),
                  pl.BlockSpec((1, ts, RET_V_W), fwd_only)],
        out_specs=pl.BlockSpec((1, ts, RET_V_W), fwd_only),
        out_shape=jax.ShapeDtypeStruct((bsz, seq, RET_V_W), BF16),
        scratch_shapes=[pltpu.VMEM((RET_HEADS, 7, RET_CHUNK, RET_CHUNK), F32),
                        pltpu.VMEM((RET_HEADS, RET_DK, RET_DV), F32),
                        pltpu.VMEM((RET_HEADS, RET_DK, RET_DV), F32),
                        pltpu.VMEM((nchunk, RET_HEADS, RET_DK, RET_DV), BF16)],
        compiler_params=_params("arbitrary", "arbitrary", "arbitrary"),
        name="retention",
    )(dec, gn_g, qk, v, rg)


def _rope64(x, cos, sin_lo, sin_hi):
    return x * cos + pltpu.roll(x, LANES - MLA_ROPE // 2, 1) * sin_lo + pltpu.roll(x, MLA_ROPE // 2, 1) * sin_hi


def _mla_prep_kernel(lat_ref, gcq_ref, wuq_ref, gckv_ref, wukv_ref, gqn_ref, gkn_ref,
                     cos_ref, slo_ref, shi_ref, qt_ref, k_ref, vt_ref):
    ts = lat_ref.shape[1]
    cos = cos_ref[...]
    slo = slo_ref[...]
    shi = shi_ref[...]

    def rms(x, g):
        ms = jnp.mean(x * x, axis=-1, keepdims=True)
        return x * lax.rsqrt(ms + EPS) * g

    cq = lat_ref[0, :, :Q_LORA].astype(F32)
    ckv = lat_ref[0, :, Q_LORA:Q_LORA + KV_LORA].astype(F32)
    kr = lat_ref[0, :, Q_LORA + KV_LORA:].astype(F32)
    cqn = rms(cq, gcq_ref[...]).astype(BF16)
    ckvn = rms(ckv, gckv_ref[...]).astype(BF16)
    gqn = gqn_ref[...]
    gkn = gkn_ref[...]
    kr = _rope64(kr, cos, slo, shi)
    kr_sq = jnp.sum(kr * kr, axis=-1, keepdims=True)
    kr_rot = _rope64(kr * gkn[:, MLA_NOPE:], cos, slo, shi)
    qscale = LOG2_E * MLA_QK ** -0.5
    for hd in range(MLA_HEADS):
        c0 = hd * MLA_QK_PAD
        qh = _dot(cqn, wuq_ref[:, c0:c0 + MLA_QK_PAD])
        ssq = jnp.sum(qh * qh, axis=-1, keepdims=True) * (1.0 / MLA_QK)
        qn = qh * lax.rsqrt(ssq + EPS) * gqn
        qh = jnp.concatenate([qn[:, :MLA_NOPE], _rope64(qn[:, MLA_NOPE:], cos, slo, shi)], axis=1) * qscale
        qt_ref[0, hd] = qh.T.astype(BF16)
        kvh = _dot(ckvn, wukv_ref[:, c0:c0 + MLA_QK_PAD])
        kn = kvh[:, :MLA_NOPE]
        ssk = (jnp.sum(kn * kn, axis=-1, keepdims=True) + kr_sq) * (1.0 / MLA_QK)
        rk = lax.rsqrt(ssk + EPS)
        k_ref[0, hd, :, :MLA_NOPE] = (kn * rk * gkn[:, :MLA_NOPE]).astype(BF16)
        k_ref[0, hd, :, MLA_NOPE:] = (kr_rot * rk).astype(BF16)
        for c in range(ts // LANES):
            vt_ref[0, hd, c] = kvh[c * LANES:(c + 1) * LANES, MLA_NOPE:].T.astype(BF16)


def _mla_prep(lat, g_cq, w_uq_p, g_ckv, w_ukv, g_qn_p, g_kn_p, cos_m, slo_m, shi_m, ts):
    bsz, seq, _ = lat.shape
    pos = pl.BlockSpec((ts, LANES), lambda b, i: (i, 0))
    head_blk = lambda w: pl.BlockSpec((1, MLA_HEADS, ts, w), lambda b, i: (b, 0, i, 0))
    return pl.pallas_call(
        _mla_prep_kernel,
        grid=(bsz, seq // ts),
        in_specs=[pl.BlockSpec((1, ts, LATENT_W), lambda b, i: (b, i, 0)),
                  _resident((1, Q_LORA)), _resident((Q_LORA, MLA_HEADS * MLA_QK_PAD)),
                  _resident((1, KV_LORA)), _resident((KV_LORA, MLA_HEADS * MLA_QK_PAD)),
                  _resident((1, MLA_QK_PAD)), _resident((1, MLA_QK_PAD)),
                  pos, pos, pos],
        out_specs=[pl.BlockSpec((1, MLA_HEADS, MLA_QK_PAD, ts), lambda b, i: (b, 0, 0, i)),
                   head_blk(MLA_QK_PAD),
                   pl.BlockSpec((1, MLA_HEADS, ts // LANES, MLA_V, LANES), lambda b, i: (b, 0, i, 0, 0))],
        out_shape=[jax.ShapeDtypeStruct((bsz, MLA_HEADS, MLA_QK_PAD, seq), BF16),
                   jax.ShapeDtypeStruct((bsz, MLA_HEADS, seq, MLA_QK_PAD), BF16),
                   jax.ShapeDtypeStruct((bsz, MLA_HEADS, seq // LANES, MLA_V, LANES), BF16)],
        compiler_params=_params("arbitrary", "arbitrary"),
        name="mla_prep",
    )(lat, g_cq, w_uq_p, g_ckv, w_ukv, g_qn_p, g_kn_p, cos_m, slo_m, shi_m)


ATT_UNROLL = 4


def _attention_kernel(qt_ref, k_ref, vt_ref, o_ref, s0_ref, s1_ref, *, tk, nkv):
    qt = qt_ref[0, 0]
    tq = qt.shape[1]
    sub = tk // LANES
    s_refs = (s0_ref, s1_ref)

    def scores(j):
        k0 = j * tk
        if not isinstance(j, int):
            k0 = pl.multiple_of(k0, tk)
        return _dot(k_ref[0, 0, pl.ds(k0, tk), :], qt)

    def update(s_ref, j, carry):
        m, l, acc = carry
        s = s_ref[...]
        m_new = jnp.maximum(m, jnp.max(s, axis=0, keepdims=True))
        p = jnp.exp2(s - m_new)
        alpha = jnp.exp2(m - m_new)
        l = alpha * l + jnp.sum(p, axis=0, keepdims=True)
        vt3 = vt_ref[0, 0, pl.ds(j * sub, sub)]
        vt = jnp.concatenate([vt3[i] for i in range(sub)], axis=1)
        acc = alpha * acc + _dot(vt, p.astype(BF16))
        return m_new, l, acc

    def trip(i, carry):
        j = i * ATT_UNROLL
        for u in range(ATT_UNROLL):
            s_refs[(u + 1) % 2][...] = scores(j + u + 1)
            carry = update(s_refs[u % 2], j + u, carry)
        return carry

    carry = (jnp.full((1, tq), -jnp.inf, F32), jnp.zeros((1, tq), F32), jnp.zeros((MLA_V, tq), F32))
    s0_ref[...] = scores(0)
    n_trips = (nkv - 1) // ATT_UNROLL
    carry = lax.fori_loop(0, n_trips, trip, carry)
    for j in range(n_trips * ATT_UNROLL, nkv):
        if j + 1 < nkv:
            s_refs[(j + 1) % 2][...] = scores(j + 1)
        carry = update(s_refs[j % 2], j, carry)
    _, l, acc = carry
    o_ref[0] = (acc / l).T.astype(BF16)


def _attention(qt, k, vt, tq, tk):
    bsz, nh, seq, _ = k.shape
    nkv = seq // tk
    assert nkv >= 2 and nkv % 2 == 0 and tk % LANES == 0, (seq, tk)
    return pl.pallas_call(
        functools.partial(_attention_kernel, tk=tk, nkv=nkv),
        grid=(bsz, nh, seq // tq),
        in_specs=[pl.BlockSpec((1, 1, MLA_QK_PAD, tq), lambda b, h, i: (b, h, 0, i)),
                  pl.BlockSpec((1, 1, seq, MLA_QK_PAD), lambda b, h, i: (b, h, 0, 0)),
                  pl.BlockSpec((1, 1, seq // LANES, MLA_V, LANES), lambda b, h, i: (b, h, 0, 0, 0))],
        out_specs=pl.BlockSpec((1, tq, MLA_V), lambda b, h, i: (b, i, h)),
        out_shape=jax.ShapeDtypeStruct((bsz, seq, nh * MLA_V), BF16),
        scratch_shapes=[pltpu.VMEM((tk, tq), F32), pltpu.VMEM((tk, tq), F32)],
        compiler_params=_params("arbitrary", "arbitrary", "arbitrary"),
        name="attention",
    )(qt, k, vt)


def _merge_kernel(x_ref, ret_ref, mla_ref, gates_ref, wr_ref, wm_ref, wo_ref, o_ref):
    ret_branch = _dot(ret_ref[...], wr_ref[...])
    mla_branch = _dot(mla_ref[...], wm_ref[...])
    merged = (gates_ref[:, :D_MODEL].astype(F32) * ret_branch
              + gates_ref[:, D_MODEL:].astype(F32) * mla_branch)
    o_ref[...] = x_ref[...] + _dot(merged.astype(BF16), wo_ref[...])


def _merge(x2, ret, mla, gates, w_ret_o, w_mla_o, w_out, tm):
    t = x2.shape[0]
    row = lambda w: pl.BlockSpec((tm, w), lambda i: (i, 0))
    sq = _resident((D_MODEL, D_MODEL))
    return pl.pallas_call(
        _merge_kernel,
        grid=(t // tm,),
        in_specs=[row(D_MODEL), row(RET_V_W), row(D_MODEL), row(2 * D_MODEL), sq, sq, sq],
        out_specs=row(D_MODEL),
        out_shape=jax.ShapeDtypeStruct((t, D_MODEL), F32),
        compiler_params=_params("arbitrary"),
        name="merge",
    )(x2, ret, mla, gates, w_ret_o, w_mla_o, w_out)


FFN_HALO = 8
FFN_NC = 256


def _ffn_kernel(x_ref, prev_ref, next_ref, g_ref, wup_ref, cw_ref, cb_ref, wdn_ref, o_ref, h_ref,
                *, ts, nblk):
    i = pl.program_id(1)
    g = g_ref[...]

    def rms(x):
        ms = jnp.mean(x * x, axis=-1, keepdims=True)
        return x * lax.rsqrt(ms + EPS) * g

    keep_prev = jnp.where(i > 0, 1.0, 0.0)
    keep_next = jnp.where(i < nblk - 1, 1.0, 0.0)
    xm = x_ref[0]
    h_ref[:FFN_HALO] = (rms(prev_ref[0]) * keep_prev).astype(BF16)
    h_ref[FFN_HALO:FFN_HALO + ts] = rms(xm).astype(BF16)
    h_ref[FFN_HALO + ts:] = (rms(next_ref[0]) * keep_next).astype(BF16)
    h = h_ref[...]

    def conv(u, c0):
        w = cw_ref[:, c0:c0 + FFN_NC]
        return (u[FFN_HALO - 1:FFN_HALO - 1 + ts] * w[0:1]
                + u[FFN_HALO:FFN_HALO + ts] * w[1:2]
                + u[FFN_HALO + 1:FFN_HALO + 1 + ts] * w[2:3]
                + cb_ref[:, c0:c0 + FFN_NC])

    acc = xm
    for n in range(D_FF // FFN_NC):
        ca = n * FFN_NC
        cg = D_FF + n * FFN_NC
        ua = conv(_dot(h, wup_ref[:, ca:ca + FFN_NC]), ca)
        ub = conv(_dot(h, wup_ref[:, cg:cg + FFN_NC]), cg)
        act = (ua * _sigmoid(ua) * ub).astype(BF16)
        acc = acc + _dot(act, wdn_ref[ca:ca + FFN_NC, :])
    o_ref[0] = acc


def _ffn(x1, g_ffn, w_up, conv_w, conv_b, w_down, ts):
    bsz, seq, _ = x1.shape
    nblk = seq // ts
    hb = ts // FFN_HALO
    last_halo = seq // FFN_HALO - 1
    return pl.pallas_call(
        functools.partial(_ffn_kernel, ts=ts, nblk=nblk),
        grid=(bsz, nblk),
        in_specs=[pl.BlockSpec((1, ts, D_MODEL), lambda b, i: (b, i, 0)),
                  pl.BlockSpec((1, FFN_HALO, D_MODEL), lambda b, i: (b, jnp.maximum(i * hb - 1, 0), 0)),
                  pl.BlockSpec((1, FFN_HALO, D_MODEL), lambda b, i: (b, jnp.minimum((i + 1) * hb, last_halo), 0)),
                  _resident((1, D_MODEL)), _resident((D_MODEL, 2 * D_FF)),
                  _resident((3, 2 * D_FF)), _resident((1, 2 * D_FF)), _resident((D_FF, D_MODEL))],
        out_specs=pl.BlockSpec((1, ts, D_MODEL), lambda b, i: (b, i, 0)),
        out_shape=jax.ShapeDtypeStruct((bsz, seq, D_MODEL), F32),
        scratch_shapes=[pltpu.VMEM((ts + 2 * FFN_HALO, D_MODEL), BF16)],
        compiler_params=_params("arbitrary", "arbitrary"),
        name="ffn",
    )(x1, x1, x1, g_ffn, w_up, conv_w, conv_b, w_down)


def _rope_tables(seq):
    pos = jnp.arange(seq).astype(F32)[:, None]

    def cos_sin(d):
        inv = ROPE_BASE ** (-jnp.arange(0, d, 2, dtype=F32) / d)
        ang = pos * inv[None, :]
        return jnp.cos(ang), jnp.sin(ang)

    cr, sr = cos_sin(RET_DK)
    cm, sm = cos_sin(MLA_ROPE)
    zh = jnp.zeros_like(sm)
    zpad = jnp.zeros((seq, LANES - MLA_ROPE), F32)
    ret = (jnp.concatenate([cr, cr], 1), jnp.concatenate([-sr, sr], 1))
    mla = (jnp.concatenate([cm, cm, zpad], 1), jnp.concatenate([-sm, zh, zpad], 1),
           jnp.concatenate([zh, sm, zpad], 1))
    return ret, mla


def _tile(seq, want):
    t = min(seq, want)
    assert seq % t == 0, (seq, t)
    return t


def _trunk(x, p, depth):
    bsz, seq, _ = x.shape
    t = bsz * seq
    (cos_r, sin_r), (cos_m, slo_m, shi_m) = _rope_tables(seq)
    tm = _tile(seq, 512)
    for l in range(depth):
        w = {k: v[l] for k, v in p.items()}
        x2 = x.reshape(t, D_MODEL)
        qk, rv, rg, lat, gates = _inproj(x2, w["g_mix"], w["w_in"], cos_r, sin_r, seq, tm)
        ret = _retention(w["dec"], w["ret_gn_g"],
                         qk.reshape(bsz, seq, -1), rv.reshape(bsz, seq, -1), rg.reshape(bsz, seq, -1),
                         _tile(seq, 512))
        q, k, v = _mla_prep(lat.reshape(bsz, seq, -1), w["g_cq"], w["w_uq"], w["g_ckv"], w["w_ukv"],
                            w["g_qn"], w["g_kn"], cos_m, slo_m, shi_m, _tile(seq, 512))
        mla = _attention(q, k, v, _tile(seq, 512), _tile(seq, 512))
        x1 = _merge(x2, ret.reshape(t, -1), mla.reshape(t, -1), gates,
                    w["w_ret_o"], w["w_mla_o"], w["w_out"], tm)
        x = _ffn(x1.reshape(bsz, seq, D_MODEL), w["g_ffn"], w["w_up"], w["conv_w"], w["conv_b"],
                 w["w_down"], _tile(seq, 512))
    return x


def _prepare_weights(g_mix, w_in, ret_decay_fwd, ret_decay_bwd, ret_gn_g, w_ret_o, g_cq, w_uq, g_ckv,
                     w_ukv, g_qn, g_kn, w_mla_o, w_out, g_ffn, w_up, conv_w, conv_b, w_down):
    depth = w_in.shape[0]
    kr_end = 2 * RET_Q_W + 2 * RET_V_W + Q_LORA + KV_LORA + MLA_ROPE
    w_in_p = jnp.concatenate(
        [w_in[:, :, :kr_end], jnp.zeros((depth, D_MODEL, LANES - MLA_ROPE), w_in.dtype), w_in[:, :, kr_end:]],
        axis=2).astype(BF16)
    pad_h = MLA_QK_PAD - MLA_QK
    w_uq_p = jnp.pad(w_uq.reshape(depth, Q_LORA, MLA_HEADS, MLA_QK), ((0, 0), (0, 0), (0, 0), (0, pad_h)))
    w_uq_p = w_uq_p.reshape(depth, Q_LORA, MLA_HEADS * MLA_QK_PAD).astype(BF16)
    dec = jnp.stack([ret_decay_fwd, ret_decay_bwd], axis=1).astype(F32)
    dec = jnp.broadcast_to(dec[..., None], (depth, 2, RET_HEADS, LANES))
    row = lambda a: a[:, None, :].astype(F32)
    return {
        "g_mix": row(g_mix), "w_in": w_in_p, "dec": dec, "ret_gn_g": row(ret_gn_g),
        "w_ret_o": w_ret_o.astype(BF16), "g_cq": row(g_cq), "w_uq": w_uq_p, "g_ckv": row(g_ckv),
        "w_ukv": w_ukv.astype(BF16),
        "g_qn": row(jnp.pad(g_qn, ((0, 0), (0, pad_h)))), "g_kn": row(jnp.pad(g_kn, ((0, 0), (0, pad_h)))),
        "w_mla_o": w_mla_o.astype(BF16), "w_out": w_out.astype(BF16), "g_ffn": row(g_ffn),
        "w_up": w_up.astype(BF16), "conv_w": conv_w.astype(F32), "conv_b": row(conv_b),
        "w_down": w_down.astype(BF16),
    }


def kernel(x_prompt, x_sample, g_mix, w_in, ret_decay_fwd, ret_decay_bwd, ret_gn_g, w_ret_o, g_cq, w_uq,
           g_ckv, w_ukv, g_qn, g_kn, w_mla_o, w_out, g_ffn, w_up, conv_w, conv_b, w_down):
    depth = w_in.shape[0]
    p = _prepare_weights(g_mix, w_in, ret_decay_fwd, ret_decay_bwd, ret_gn_g, w_ret_o, g_cq, w_uq, g_ckv,
                         w_ukv, g_qn, g_kn, w_mla_o, w_out, g_ffn, w_up, conv_w, conv_b, w_down)
    return (_trunk(x_prompt, p, depth), _trunk(x_sample, p, depth))
```

```python
import functools

import jax
import jax.numpy as jnp
from jax import lax
from jax.experimental import pallas as pl
from jax.experimental.pallas import tpu as pltpu

D_MODEL = 1024
RET_HEADS = 4
RET_DK = 128
RET_DV = 256
RET_CHUNK = 128
MLA_HEADS = 8
MLA_NOPE = 128
MLA_ROPE = 64
MLA_V = 128
MLA_QK = MLA_NOPE + MLA_ROPE
MLA_QK_PAD = 256
Q_LORA = 384
KV_LORA = 256
D_FF = 2816
ROPE_BASE = 10000.0
EPS = 1e-6
LOG2_E = 1.4426950408889634

RET_Q_W = RET_HEADS * RET_DK
RET_V_W = RET_HEADS * RET_DV
LATENT_W = Q_LORA + KV_LORA + 128
IN_W_PAD = 2 * RET_Q_W + 2 * RET_V_W + LATENT_W + 2 * D_MODEL

LANES = 128
VMEM_LIMIT_BYTES = 56 * 1024 * 1024

F32 = jnp.float32
BF16 = jnp.bfloat16


def _params(*semantics):
    return pltpu.CompilerParams(dimension_semantics=semantics, vmem_limit_bytes=VMEM_LIMIT_BYTES)


def _resident(shape):
    return pl.BlockSpec(shape, lambda *_: (0,) * len(shape), pipeline_mode=pl.Buffered(1))


def _sigmoid(x):
    return 1.0 / (1.0 + jnp.exp(-x))


def _dot(a, b):
    return jnp.dot(a, b, preferred_element_type=F32)


def _dot_nt(a, b):
    return lax.dot_general(a, b, (((1,), (1,)), ((), ())), preferred_element_type=F32)


def _dot_tn(a, b):
    return lax.dot_general(a, b, (((0,), (0,)), ((), ())), preferred_element_type=F32)


def _inproj_kernel(x_ref, g_ref, w_ref, cos_ref, sin_ref,
                   qk_ref, v_ref, rg_ref, lat_ref, gates_ref):
    xf = x_ref[...]
    ms = jnp.mean(xf * xf, axis=-1, keepdims=True)
    h = (xf * lax.rsqrt(ms + EPS) * g_ref[...]).astype(BF16)
    cos = cos_ref[...]
    sin = sin_ref[...]

    def proj(c0, width):
        return _dot(h, w_ref[:, c0:c0 + width])

    for seg, scale in ((0, None), (1, RET_DK ** -0.5)):
        acc = proj(seg * RET_Q_W, RET_Q_W)
        for hd in range(RET_HEADS):
            xs = acc[:, hd * RET_DK:(hd + 1) * RET_DK]
            r = xs * cos + pltpu.roll(xs, RET_DK // 2, 1) * sin
            if scale is not None:
                r = r * scale
            c0 = seg * RET_Q_W + hd * RET_DK
            qk_ref[:, c0:c0 + RET_DK] = r.astype(BF16)
    base = 2 * RET_Q_W
    half = RET_V_W // 2
    for j in range(2):
        v_ref[:, j * half:(j + 1) * half] = proj(base + j * half, half).astype(BF16)
    base += RET_V_W
    for j in range(2):
        a = proj(base + j * half, half)
        rg_ref[:, j * half:(j + 1) * half] = (a * _sigmoid(a)).astype(BF16)
    base += RET_V_W
    lat_ref[...] = proj(base, LATENT_W).astype(BF16)
    base += LATENT_W
    for j in range(4):
        a = proj(base + j * half, half)
        gates_ref[:, j * half:(j + 1) * half] = _sigmoid(a).astype(BF16)


def _inproj(x2, g_mix, w_in_p, cos_r, sin_r, seq, tm):
    t = x2.shape[0]
    nseq = seq // tm
    row = lambda w: pl.BlockSpec((tm, w), lambda i: (i, 0))
    pos = pl.BlockSpec((tm, LANES), lambda i: (i % nseq, 0))
    out_w = (2 * RET_Q_W, RET_V_W, RET_V_W, LATENT_W, 2 * D_MODEL)
    return pl.pallas_call(
        _inproj_kernel,
        grid=(t // tm,),
        in_specs=[row(D_MODEL), _resident((1, D_MODEL)), _resident((D_MODEL, IN_W_PAD)), pos, pos],
        out_specs=[row(w) for w in out_w],
        out_shape=[jax.ShapeDtypeStruct((t, w), BF16) for w in out_w],
        compiler_params=_params("arbitrary"),
        name="inproj",
    )(x2, g_mix, w_in_p, cos_r, sin_r)


_T_DMAT, _T_QF, _T_QB, _T_KF, _T_KB, _T_CF, _T_CB = range(7)


def _retention_kernel(dec_ref, gn_ref, qk_ref, v_ref, rg_ref, o_ref,
                      tab_ref, sf_ref, sb_ref, sball_ref, *, ts, nblk):
    c = RET_CHUNK
    ncb = ts // c
    b = pl.program_id(0)
    ph = pl.program_id(1)
    i = pl.program_id(2)

    @pl.when((b == 0) & (ph == 0) & (i == 0))
    def _tables():
        row = lax.broadcasted_iota(jnp.int32, (c, c), 0).astype(F32)
        col = lax.broadcasted_iota(jnp.int32, (c, c), 1).astype(F32)
        diff = row - col
        for hd in range(RET_HEADS):
            def log_sigmoid(d):
                return jnp.minimum(d, 0.0) - jnp.log1p(jnp.exp(-jnp.abs(d)))
            lgf = log_sigmoid(dec_ref[0, hd:hd + 1, :])
            lgb = log_sigmoid(dec_ref[1, hd:hd + 1, :])
            tab_ref[hd, _T_DMAT] = jnp.where(diff >= 0.0,
                                             jnp.exp(lgf * jnp.maximum(diff, 0.0)),
                                             jnp.exp(lgb * jnp.maximum(-diff, 0.0)))
            tab_ref[hd, _T_QF] = jnp.exp(lgf * (row + 1.0))
            tab_ref[hd, _T_QB] = jnp.exp(lgb * (c - row))
            tab_ref[hd, _T_KF] = jnp.exp(lgf * (c - 1.0 - row))
            tab_ref[hd, _T_KB] = jnp.exp(lgb * row)
            tab_ref[hd, _T_CF] = jnp.exp(lgf * (row * 0.0 + c))
            tab_ref[hd, _T_CB] = jnp.exp(lgb * (row * 0.0 + c))

    def chunk_decay(hd, which):
        t = tab_ref[hd, which]
        return jnp.concatenate([t, t], axis=1)

    @pl.when(ph == 0)
    def _backward_states():
        @pl.when(i == 0)
        def _():
            sb_ref[...] = jnp.zeros_like(sb_ref)
        blk = nblk - 1 - i
        for cb in reversed(range(ncb)):
            r0 = cb * c
            for hd in range(RET_HEADS):
                k = qk_ref[0, r0:r0 + c, RET_Q_W + hd * RET_DK:RET_Q_W + (hd + 1) * RET_DK]
                v = v_ref[0, r0:r0 + c, hd * RET_DV:(hd + 1) * RET_DV]
                st = sb_ref[hd]
                sball_ref[blk * ncb + cb, hd] = st.astype(BF16)
                kd = (k.astype(F32) * tab_ref[hd, _T_KB]).astype(BF16)
                sb_ref[hd] = st * chunk_decay(hd, _T_CB) + _dot_tn(kd, v)

    @pl.when(ph == 1)
    def _forward():
        @pl.when(i == 0)
        def _():
            sf_ref[...] = jnp.zeros_like(sf_ref)
        for cb in range(ncb):
            r0 = cb * c
            for hd in range(RET_HEADS):
                q = qk_ref[0, r0:r0 + c, hd * RET_DK:(hd + 1) * RET_DK]
                k = qk_ref[0, r0:r0 + c, RET_Q_W + hd * RET_DK:RET_Q_W + (hd + 1) * RET_DK]
                v = v_ref[0, r0:r0 + c, hd * RET_DV:(hd + 1) * RET_DV]
                qf32 = q.astype(F32)
                kf32 = k.astype(F32)
                sd = (_dot_nt(q, k) * tab_ref[hd, _T_DMAT]).astype(BF16)
                st = sf_ref[hd]
                out = _dot(sd, v)
                out = out + _dot((qf32 * tab_ref[hd, _T_QF]).astype(BF16), st.astype(BF16))
                out = out + _dot((qf32 * tab_ref[hd, _T_QB]).astype(BF16), sball_ref[i * ncb + cb, hd])
                kd = (kf32 * tab_ref[hd, _T_KF]).astype(BF16)
                sf_ref[hd] = st * chunk_decay(hd, _T_CF) + _dot_tn(kd, v)
                mu = jnp.mean(out, axis=-1, keepdims=True)
                xc = out - mu
                var = jnp.mean(xc * xc, axis=-1, keepdims=True)
                y = xc * lax.rsqrt(var + EPS) * gn_ref[:, hd * RET_DV:(hd + 1) * RET_DV]
                gate = rg_ref[0, r0:r0 + c, hd * RET_DV:(hd + 1) * RET_DV].astype(F32)
                o_ref[0, r0:r0 + c, hd * RET_DV:(hd + 1) * RET_DV] = (y * gate).astype(BF16)


def _retention(dec, gn_g, qk, v, rg, ts):
    bsz, seq, _ = qk.shape
    nblk = seq // ts
    nchunk = seq // RET_CHUNK
    sweep = lambda b, ph, i: (b, i * ph + (nblk - 1 - i) * (1 - ph), 0)
    fwd_only = lambda b, ph, i: (b, i * ph, 0)
    return pl.pallas_call(
        functools.partial(_retention_kernel, ts=ts, nblk=nblk),
        grid=(bsz, 2, nblk),
        in_specs=[_resident((2, RET_HEADS, LANES)), _resident((1, RET_V_W)),
                  pl.BlockSpec((1, ts, 2 * RET_Q_W), sweep),
                  pl.BlockSpec((1, ts, RET_V_W), sweep),
                  pl.BlockSpec((1, ts, RET_V_W), fwd_only)],
        out_specs=pl.BlockSpec((1, ts, RET_V_W), fwd_only),
        out_shape=jax.ShapeDtypeStruct((bsz, seq, RET_V_W), BF16),
        scratch_shapes=[pltpu.VMEM((RET_HEADS, 7, RET_CHUNK, RET_CHUNK), F32),
                        pltpu.VMEM((RET_HEADS, RET_DK, RET_DV), F32),
                        pltpu.VMEM((RET_HEADS, RET_DK, RET_DV), F32),
                        pltpu.VMEM((nchunk, RET_HEADS, RET_DK, RET_DV), BF16)],
        compiler_params=_params("arbitrary", "arbitrary", "arbitrary"),
        name="retention",
    )(dec, gn_g, qk, v, rg)


def _rope64(x, cos, sin_lo, sin_hi):
    return x * cos + pltpu.roll(x, LANES - MLA_ROPE // 2, 1) * sin_lo + pltpu.roll(x, MLA_ROPE // 2, 1) * sin_hi


def _mla_prep_kernel(lat_ref, gcq_ref, wuq_ref, gckv_ref, wukv_ref, gqn_ref, gkn_ref,
                     cos_ref, slo_ref, shi_ref, qt_ref, k_ref, vt_ref):
    ts = lat_ref.shape[1]
    cos = cos_ref[...]
    slo = slo_ref[...]
    shi = shi_ref[...]

    def rms(x, g):
        ms = jnp.mean(x * x, axis=-1, keepdims=True)
        return x * lax.rsqrt(ms + EPS) * g

    cq = lat_ref[0, :, :Q_LORA].astype(F32)
    ckv = lat_ref[0, :, Q_LORA:Q_LORA + KV_LORA].astype(F32)
    kr = lat_ref[0, :, Q_LORA + KV_LORA:].astype(F32)
    cqn = rms(cq, gcq_ref[...]).astype(BF16)
    ckvn = rms(ckv, gckv_ref[...]).astype(BF16)
    gqn = gqn_ref[...]
    gkn = gkn_ref[...]
    kr = _rope64(kr, cos, slo, shi)
    kr_sq = jnp.sum(kr * kr, axis=-1, keepdims=True)
    kr_rot = _rope64(kr * gkn[:, MLA_NOPE:], cos, slo, shi)
    qscale = LOG2_E * MLA_QK ** -0.5
    for hd in range(MLA_HEADS):
        c0 = hd * MLA_QK_PAD
        qh = _dot(cqn, wuq_ref[:, c0:c0 + MLA_QK_PAD])
        ssq = jnp.sum(qh * qh, axis=-1, keepdims=True) * (1.0 / MLA_QK)
        qn = qh * lax.rsqrt(ssq + EPS) * gqn
        qh = jnp.concatenate([qn[:, :MLA_NOPE], _rope64(qn[:, MLA_NOPE:], cos, slo, shi)], axis=1) * qscale
        qt_ref[0, hd] = qh.T.astype(BF16)
        kvh = _dot(ckvn, wukv_ref[:, c0:c0 + MLA_QK_PAD])
        kn = kvh[:, :MLA_NOPE]
        ssk = (jnp.sum(kn * kn, axis=-1, keepdims=True) + kr_sq) * (1.0 / MLA_QK)
        rk = lax.rsqrt(ssk + EPS)
        k_ref[0, hd, :, :MLA_NOPE] = (kn * rk * gkn[:, :MLA_NOPE]).astype(BF16)
        k_ref[0, hd, :, MLA_NOPE:] = (kr_rot * rk).astype(BF16)
        for c in range(ts // LANES):
            vt_ref[0, hd, c] = kvh[c * LANES:(c + 1) * LANES, MLA_NOPE:].T.astype(BF16)


def _mla_prep(lat, g_cq, w_uq_p, g_ckv, w_ukv, g_qn_p, g_kn_p, cos_m, slo_m, shi_m, ts):
    bsz, seq, _ = lat.shape
    pos = pl.BlockSpec((ts, LANES), lambda b, i: (i, 0))
    head_blk = lambda w: pl.BlockSpec((1, MLA_HEADS, ts, w), lambda b, i: (b, 0, i, 0))
    return pl.pallas_call(
        _mla_prep_kernel,
        grid=(bsz, seq // ts),
        in_specs=[pl.BlockSpec((1, ts, LATENT_W), lambda b, i: (b, i, 0)),
                  _resident((1, Q_LORA)), _resident((Q_LORA, MLA_HEADS * MLA_QK_PAD)),
                  _resident((1, KV_LORA)), _resident((KV_LORA, MLA_HEADS * MLA_QK_PAD)),
                  _resident((1, MLA_QK_PAD)), _resident((1, MLA_QK_PAD)),
                  pos, pos, pos],
        out_specs=[pl.BlockSpec((1, MLA_HEADS, MLA_QK_PAD, ts), lambda b, i: (b, 0, 0, i)),
                   head_blk(MLA_QK_PAD),
                   pl.BlockSpec((1, MLA_HEADS, ts // LANES, MLA_V, LANES), lambda b, i: (b, 0, i, 0, 0))],
        out_shape=[jax.ShapeDtypeStruct((bsz, MLA_HEADS, MLA_QK_PAD, seq), BF16),
                   jax.ShapeDtypeStruct((bsz, MLA_HEADS, seq, MLA_QK_PAD), BF16),
                   jax.ShapeDtypeStruct((bsz, MLA_HEADS, seq // LANES, MLA_V, LANES), BF16)],
        compiler_params=_params("arbitrary", "arbitrary"),
        name="mla_prep",
    )(lat, g_cq, w_uq_p, g_ckv, w_ukv, g_qn_p, g_kn_p, cos_m, slo_m, shi_m)


ATT_UNROLL = 4
ATT_PITCH_PAD = LANES
ATT_FIXED_SHIFT_MAX = 60.0


def _attention_kernel(qt_ref, k_ref, vt_ref, o_ref, s0_ref, s1_ref, kmax_ref, *, tk, nkv):
    qi = pl.program_id(2)
    qt = qt_ref[0, 0]
    tq = qt.shape[1]
    sub = tk // LANES
    s_refs = (s0_ref, s1_ref)

    def key_block(j):
        k0 = j * tk
        if not isinstance(j, int):
            k0 = pl.multiple_of(k0, tk)
        return k_ref[0, 0, pl.ds(k0, tk), :]

    def scores(j):
        return _dot(key_block(j), qt)

    def values_t(j):
        vt3 = vt_ref[0, 0, pl.ds(j * sub, sub)]
        return jnp.concatenate([vt3[i] for i in range(sub)], axis=1)

    def blocks(n_trips, trip_fn, tail_fn, carry):
        carry = lax.fori_loop(0, n_trips, trip_fn, carry)
        for j in range(n_trips * ATT_UNROLL, nkv):
            carry = tail_fn(j, carry)
        return carry

    @pl.when(qi == 0)
    def _largest_key_norm():
        ones = jnp.ones((MLA_QK_PAD, LANES), BF16)

        def body(j, mx):
            kc = key_block(j).astype(F32)
            return jnp.maximum(mx, _dot((kc * kc).astype(BF16), ones))

        mx = lax.fori_loop(0, nkv, body, jnp.zeros((tk, LANES), F32))
        kmax_ref[...] = jnp.max(mx, axis=0, keepdims=True)

    q32 = qt.astype(F32)
    qn2 = jnp.sum(q32 * q32, axis=0, keepdims=True)
    kn2 = jnp.concatenate([kmax_ref[...]] * (tq // LANES), axis=1)
    shift = jnp.sqrt(qn2 * kn2)
    fixed_ok = jnp.max(shift) <= ATT_FIXED_SHIFT_MAX

    @pl.when(fixed_ok)
    def _fixed_shift():
        def block(j, carry):
            l, acc = carry
            e = jnp.exp2(scores(j) - shift)
            return l + jnp.sum(e, axis=0, keepdims=True), acc + _dot(values_t(j), e.astype(BF16))

        def trip(i, carry):
            for u in range(ATT_UNROLL):
                carry = block(i * ATT_UNROLL + u, carry)
            return carry

        carry = (jnp.zeros((1, tq), F32), jnp.zeros((MLA_V, tq), F32))
        l, acc = blocks(nkv // ATT_UNROLL, trip, block, carry)
        o_ref[0] = (acc / l).T.astype(BF16)

    @pl.when(jnp.logical_not(fixed_ok))
    def _running_max():
        def update(s_ref, j, carry):
            m, l, acc = carry
            s = s_ref[:, :tq]
            m_new = jnp.maximum(m, jnp.max(s, axis=0, keepdims=True))
            p = jnp.exp2(s - m_new)
            alpha = jnp.exp2(m - m_new)
            l = alpha * l + jnp.sum(p, axis=0, keepdims=True)
            acc = alpha * acc + _dot(values_t(j), p.astype(BF16))
            return m_new, l, acc

        def trip(i, carry):
            j = i * ATT_UNROLL
            for u in range(ATT_UNROLL):
                s_refs[(u + 1) % 2][:, :tq] = scores(j + u + 1)
                carry = update(s_refs[u % 2], j + u, carry)
            return carry

        def tail(j, carry):
            if j + 1 < nkv:
                s_refs[(j + 1) % 2][:, :tq] = scores(j + 1)
            return update(s_refs[j % 2], j, carry)

        carry = (jnp.full((1, tq), -jnp.inf, F32), jnp.zeros((1, tq), F32), jnp.zeros((MLA_V, tq), F32))
        s0_ref[:, :tq] = scores(0)
        _, l, acc = blocks((nkv - 1) // ATT_UNROLL, trip, tail, carry)
        o_ref[0] = (acc / l).T.astype(BF16)


def _attention(qt, k, vt, tq, tk):
    bsz, nh, seq, _ = k.shape
    nkv = seq // tk
    assert nkv >= 2 and nkv % 2 == 0 and tk % LANES == 0, (seq, tk)
    return pl.pallas_call(
        functools.partial(_attention_kernel, tk=tk, nkv=nkv),
        grid=(bsz, nh, seq // tq),
        in_specs=[pl.BlockSpec((1, 1, MLA_QK_PAD, tq), lambda b, h, i: (b, h, 0, i)),
                  pl.BlockSpec((1, 1, seq, MLA_QK_PAD), lambda b, h, i: (b, h, 0, 0)),
                  pl.BlockSpec((1, 1, seq // LANES, MLA_V, LANES), lambda b, h, i: (b, h, 0, 0, 0))],
        out_specs=pl.BlockSpec((1, tq, MLA_V), lambda b, h, i: (b, i, h)),
        out_shape=jax.ShapeDtypeStruct((bsz, seq, nh * MLA_V), BF16),
        scratch_shapes=[pltpu.VMEM((tk, tq + ATT_PITCH_PAD), F32), pltpu.VMEM((tk, tq + ATT_PITCH_PAD), F32),
                        pltpu.VMEM((1, LANES), F32)],
        compiler_params=_params("arbitrary", "arbitrary", "arbitrary"),
        name="attention",
    )(qt, k, vt)


def _merge_kernel(x_ref, ret_ref, mla_ref, gates_ref, wr_ref, wm_ref, wo_ref, o_ref):
    ret_branch = _dot(ret_ref[...], wr_ref[...])
    mla_branch = _dot(mla_ref[...], wm_ref[...])
    merged = (gates_ref[:, :D_MODEL].astype(F32) * ret_branch
              + gates_ref[:, D_MODEL:].astype(F32) * mla_branch)
    o_ref[...] = x_ref[...] + _dot(merged.astype(BF16), wo_ref[...])


def _merge(x2, ret, mla, gates, w_ret_o, w_mla_o, w_out, tm):
    t = x2.shape[0]
    row = lambda w: pl.BlockSpec((tm, w), lambda i: (i, 0))
    sq = _resident((D_MODEL, D_MODEL))
    return pl.pallas_call(
        _merge_kernel,
        grid=(t // tm,),
        in_specs=[row(D_MODEL), row(RET_V_W), row(D_MODEL), row(2 * D_MODEL), sq, sq, sq],
        out_specs=row(D_MODEL),
        out_shape=jax.ShapeDtypeStruct((t, D_MODEL), F32),
        compiler_params=_params("arbitrary"),
        name="merge",
    )(x2, ret, mla, gates, w_ret_o, w_mla_o, w_out)


FFN_HALO = 8
FFN_NC = 256


def _ffn_kernel(x_ref, prev_ref, next_ref, g_ref, wup_ref, cw_ref, cb_ref, wdn_ref, o_ref, h_ref,
                *, ts, nblk):
    i = pl.program_id(1)
    g = g_ref[...]

    def rms(x):
        ms = jnp.mean(x * x, axis=-1, keepdims=True)
        return x * lax.rsqrt(ms + EPS) * g

    keep_prev = jnp.where(i > 0, 1.0, 0.0)
    keep_next = jnp.where(i < nblk - 1, 1.0, 0.0)
    xm = x_ref[0]
    h_ref[:FFN_HALO] = (rms(prev_ref[0]) * keep_prev).astype(BF16)
    h_ref[FFN_HALO:FFN_HALO + ts] = rms(xm).astype(BF16)
    h_ref[FFN_HALO + ts:] = (rms(next_ref[0]) * keep_next).astype(BF16)
    h = h_ref[...]

    def conv(u, c0):
        w = cw_ref[:, c0:c0 + FFN_NC]
        return (u[FFN_HALO - 1:FFN_HALO - 1 + ts] * w[0:1]
                + u[FFN_HALO:FFN_HALO + ts] * w[1:2]
                + u[FFN_HALO + 1:FFN_HALO + 1 + ts] * w[2:3]
                + cb_ref[:, c0:c0 + FFN_NC])

    acc = xm
    for n in range(D_FF // FFN_NC):
        ca = n * FFN_NC
        cg = D_FF + n * FFN_NC
        ua = conv(_dot(h, wup_ref[:, ca:ca + FFN_NC]), ca)
        ub = conv(_dot(h, wup_ref[:, cg:cg + FFN_NC]), cg)
        act = (ua * _sigmoid(ua) * ub).astype(BF16)
        acc = acc + _dot(act, wdn_ref[ca:ca + FFN_NC, :])
    o_ref[0] = acc


def _ffn(x1, g_ffn, w_up, conv_w, conv_b, w_down, ts):
    bsz, seq, _ = x1.shape
    nblk = seq // ts
    hb = ts // FFN_HALO
    last_halo = seq // FFN_HALO - 1
    return pl.pallas_call(
        functools.partial(_ffn_kernel, ts=ts, nblk=nblk),
        grid=(bsz, nblk),
        in_specs=[pl.BlockSpec((1, ts, D_MODEL), lambda b, i: (b, i, 0)),
                  pl.BlockSpec((1, FFN_HALO, D_MODEL), lambda b, i: (b, jnp.maximum(i * hb - 1, 0), 0)),
                  pl.BlockSpec((1, FFN_HALO, D_MODEL), lambda b, i: (b, jnp.minimum((i + 1) * hb, last_halo), 0)),
                  _resident((1, D_MODEL)), _resident((D_MODEL, 2 * D_FF)),
                  _resident((3, 2 * D_FF)), _resident((1, 2 * D_FF)), _resident((D_FF, D_MODEL))],
        out_specs=pl.BlockSpec((1, ts, D_MODEL), lambda b, i: (b, i, 0)),
        out_shape=jax.ShapeDtypeStruct((bsz, seq, D_MODEL), F32),
        scratch_shapes=[pltpu.VMEM((ts + 2 * FFN_HALO, D_MODEL), BF16)],
        compiler_params=_params("arbitrary", "arbitrary"),
        name="ffn",
    )(x1, x1, x1, g_ffn, w_up, conv_w, conv_b, w_down)


def _rope_tables(seq):
    pos = jnp.arange(seq).astype(F32)[:, None]

    def cos_sin(d):
        inv = ROPE_BASE ** (-jnp.arange(0, d, 2, dtype=F32) / d)
        ang = pos * inv[None, :]
        return jnp.cos(ang), jnp.sin(ang)

    cr, sr = cos_sin(RET_DK)
    cm, sm = cos_sin(MLA_ROPE)
    zh = jnp.zeros_like(sm)
    zpad = jnp.zeros((seq, LANES - MLA_ROPE), F32)
    ret = (jnp.concatenate([cr, cr], 1), jnp.concatenate([-sr, sr], 1))
    mla = (jnp.concatenate([cm, cm, zpad], 1), jnp.concatenate([-sm, zh, zpad], 1),
           jnp.concatenate([zh, sm, zpad], 1))
    return ret, mla


def _tile(seq, want):
    t = min(seq, want)
    assert seq % t == 0, (seq, t)
    return t


def _trunk(x, p, depth):
    bsz, seq, _ = x.shape
    t = bsz * seq
    (cos_r, sin_r), (cos_m, slo_m, shi_m) = _rope_tables(seq)
    tm = _tile(seq, 512)
    for l in range(depth):
        w = {k: v[l] for k, v in p.items()}
        x2 = x.reshape(t, D_MODEL)
        qk, rv, rg, lat, gates = _inproj(x2, w["g_mix"], w["w_in"], cos_r, sin_r, seq, tm)
        ret = _retention(w["dec"], w["ret_gn_g"],
                         qk.reshape(bsz, seq, -1), rv.reshape(bsz, seq, -1), rg.reshape(bsz, seq, -1),
                         _tile(seq, 512))
        q, k, v = _mla_prep(lat.reshape(bsz, seq, -1), w["g_cq"], w["w_uq"], w["g_ckv"], w["w_ukv"],
                            w["g_qn"], w["g_kn"], cos_m, slo_m, shi_m, _tile(seq, 512))
        mla = _attention(q, k, v, _tile(seq, 1024), _tile(seq, 512))
        x1 = _merge(x2, ret.reshape(t, -1), mla.reshape(t, -1), gates,
                    w["w_ret_o"], w["w_mla_o"], w["w_out"], tm)
        x = _ffn(x1.reshape(bsz, seq, D_MODEL), w["g_ffn"], w["w_up"], w["conv_w"], w["conv_b"],
                 w["w_down"], _tile(seq, 512))
    return x


def _prepare_weights(g_mix, w_in, ret_decay_fwd, ret_decay_bwd, ret_gn_g, w_ret_o, g_cq, w_uq, g_ckv,
                     w_ukv, g_qn, g_kn, w_mla_o, w_out, g_ffn, w_up, conv_w, conv_b, w_down):
    depth = w_in.shape[0]
    kr_end = 2 * RET_Q_W + 2 * RET_V_W + Q_LORA + KV_LORA + MLA_ROPE
    w_in_p = jnp.concatenate(
        [w_in[:, :, :kr_end], jnp.zeros((depth, D_MODEL, LANES - MLA_ROPE), w_in.dtype), w_in[:, :, kr_end:]],
        axis=2).astype(BF16)
    pad_h = MLA_QK_PAD - MLA_QK
    w_uq_p = jnp.pad(w_uq.reshape(depth, Q_LORA, MLA_HEADS, MLA_QK), ((0, 0), (0, 0), (0, 0), (0, pad_h)))
    w_uq_p = w_uq_p.reshape(depth, Q_LORA, MLA_HEADS * MLA_QK_PAD).astype(BF16)
    dec = jnp.stack([ret_decay_fwd, ret_decay_bwd], axis=1).astype(F32)
    dec = jnp.broadcast_to(dec[..., None], (depth, 2, RET_HEADS, LANES))
    row = lambda a: a[:, None, :].astype(F32)
    return {
        "g_mix": row(g_mix), "w_in": w_in_p, "dec": dec, "ret_gn_g": row(ret_gn_g),
        "w_ret_o": w_ret_o.astype(BF16), "g_cq": row(g_cq), "w_uq": w_uq_p, "g_ckv": row(g_ckv),
        "w_ukv": w_ukv.astype(BF16),
        "g_qn": row(jnp.pad(g_qn, ((0, 0), (0, pad_h)))), "g_kn": row(jnp.pad(g_kn, ((0, 0), (0, pad_h)))),
        "w_mla_o": w_mla_o.astype(BF16), "w_out": w_out.astype(BF16), "g_ffn": row(g_ffn),
        "w_up": w_up.astype(BF16), "conv_w": conv_w.astype(F32), "conv_b": row(conv_b),
        "w_down": w_down.astype(BF16),
    }


def kernel(x_prompt, x_sample, g_mix, w_in, ret_decay_fwd, ret_decay_bwd, ret_gn_g, w_ret_o, g_cq, w_uq,
           g_ckv, w_ukv, g_qn, g_kn, w_mla_o, w_out, g_ffn, w_up, conv_w, conv_b, w_down):
    depth = w_in.shape[0]
    p = _prepare_weights(g_mix, w_in, ret_decay_fwd, ret_decay_bwd, ret_gn_g, w_ret_o, g_cq, w_uq, g_ckv,
                         w_ukv, g_qn, g_kn, w_mla_o, w_out, g_ffn, w_up, conv_w, conv_b, w_down)
    return (_trunk(x_prompt, p, depth), _trunk(x_sample, p, depth))
```

```python
import functools

import jax
import jax.numpy as jnp
from jax import lax
from jax.experimental import pallas as pl
from jax.experimental.pallas import tpu as pltpu

D_MODEL = 1024
RET_HEADS = 4
RET_DK = 128
RET_DV = 256
RET_CHUNK = 128
MLA_HEADS = 8
MLA_NOPE = 128
MLA_ROPE = 64
MLA_V = 128
MLA_QK = MLA_NOPE + MLA_ROPE
MLA_QK_PAD = 256
Q_LORA = 384
KV_LORA = 256
D_FF = 2816
ROPE_BASE = 10000.0
EPS = 1e-6
LOG2_E = 1.4426950408889634

RET_Q_W = RET_HEADS * RET_DK
RET_V_W = RET_HEADS * RET_DV
LATENT_W = Q_LORA + KV_LORA + 128
IN_W_PAD = 2 * RET_Q_W + 2 * RET_V_W + LATENT_W + 2 * D_MODEL

LANES = 128
VMEM_LIMIT_BYTES = 56 * 1024 * 1024

F32 = jnp.float32
BF16 = jnp.bfloat16


def _params(*semantics):
    return pltpu.CompilerParams(dimension_semantics=semantics, vmem_limit_bytes=VMEM_LIMIT_BYTES)


def _resident(shape):
    return pl.BlockSpec(shape, lambda *_: (0,) * len(shape), pipeline_mode=pl.Buffered(1))


def _sigmoid(x):
    return 1.0 / (1.0 + jnp.exp(-x))


def _dot(a, b):
    return jnp.dot(a, b, preferred_element_type=F32)


def _dot_nt(a, b):
    return lax.dot_general(a, b, (((1,), (1,)), ((), ())), preferred_element_type=F32)


def _dot_tn(a, b):
    return lax.dot_general(a, b, (((0,), (0,)), ((), ())), preferred_element_type=F32)


def _inproj_kernel(x_ref, g_ref, w_ref, cos_ref, sin_ref,
                   qk_ref, v_ref, rg_ref, lat_ref, gates_ref):
    xf = x_ref[...]
    ms = jnp.mean(xf * xf, axis=-1, keepdims=True)
    h = (xf * lax.rsqrt(ms + EPS) * g_ref[...]).astype(BF16)
    cos = cos_ref[...]
    sin = sin_ref[...]

    def proj(c0, width):
        return _dot(h, w_ref[:, c0:c0 + width])

    for seg, scale in ((0, None), (1, RET_DK ** -0.5)):
        acc = proj(seg * RET_Q_W, RET_Q_W)
        for hd in range(RET_HEADS):
            xs = acc[:, hd * RET_DK:(hd + 1) * RET_DK]
            r = xs * cos + pltpu.roll(xs, RET_DK // 2, 1) * sin
            if scale is not None:
                r = r * scale
            c0 = seg * RET_Q_W + hd * RET_DK
            qk_ref[:, c0:c0 + RET_DK] = r.astype(BF16)
    base = 2 * RET_Q_W
    half = RET_V_W // 2
    for j in range(2):
        v_ref[:, j * half:(j + 1) * half] = proj(base + j * half, half).astype(BF16)
    base += RET_V_W
    for j in range(2):
        a = proj(base + j * half, half)
        rg_ref[:, j * half:(j + 1) * half] = (a * _sigmoid(a)).astype(BF16)
    base += RET_V_W
    lat_ref[...] = proj(base, LATENT_W).astype(BF16)
    base += LATENT_W
    for j in range(4):
        a = proj(base + j * half, half)
        gates_ref[:, j * half:(j + 1) * half] = _sigmoid(a).astype(BF16)


def _inproj(x2, g_mix, w_in_p, cos_r, sin_r, seq, tm):
    t = x2.shape[0]
    nseq = seq // tm
    row = lambda w: pl.BlockSpec((tm, w), lambda i: (i, 0))
    pos = pl.BlockSpec((tm, LANES), lambda i: (i % nseq, 0))
    out_w = (2 * RET_Q_W, RET_V_W, RET_V_W, LATENT_W, 2 * D_MODEL)
    return pl.pallas_call(
        _inproj_kernel,
        grid=(t // tm,),
        in_specs=[row(D_MODEL), _resident((1, D_MODEL)), _resident((D_MODEL, IN_W_PAD)), pos, pos],
        out_specs=[row(w) for w in out_w],
        out_shape=[jax.ShapeDtypeStruct((t, w), BF16) for w in out_w],
        compiler_params=_params("arbitrary"),
        name="inproj",
    )(x2, g_mix, w_in_p, cos_r, sin_r)


_T_DMAT, _T_QF, _T_QB, _T_KF, _T_KB, _T_CF, _T_CB = range(7)


def _retention_kernel(dec_ref, gn_ref, qk_ref, v_ref, rg_ref, o_ref,
                      tab_ref, sf_ref, sb_ref, sball_ref, *, ts, nblk):
    c = RET_CHUNK
    ncb = ts // c
    b = pl.program_id(0)
    ph = pl.program_id(1)
    i = pl.program_id(2)

    @pl.when((b == 0) & (ph == 0) & (i == 0))
    def _tables():
        row = lax.broadcasted_iota(jnp.int32, (c, c), 0).astype(F32)
        col = lax.broadcasted_iota(jnp.int32, (c, c), 1).astype(F32)
        diff = row - col
        for hd in range(RET_HEADS):
            def log_sigmoid(d):
                return jnp.minimum(d, 0.0) - jnp.log1p(jnp.exp(-jnp.abs(d)))
            lgf = log_sigmoid(dec_ref[0, hd:hd + 1, :])
            lgb = log_sigmoid(dec_ref[1, hd:hd + 1, :])
            tab_ref[hd, _T_DMAT] = jnp.where(diff >= 0.0,
                                             jnp.exp(lgf * jnp.maximum(diff, 0.0)),
                                             jnp.exp(lgb * jnp.maximum(-diff, 0.0)))
            tab_ref[hd, _T_QF] = jnp.exp(lgf * (row + 1.0))
            tab_ref[hd, _T_QB] = jnp.exp(lgb * (c - row))
            tab_ref[hd, _T_KF] = jnp.exp(lgf * (c - 1.0 - row))
            tab_ref[hd, _T_KB] = jnp.exp(lgb * row)
            tab_ref[hd, _T_CF] = jnp.exp(lgf * (row * 0.0 + c))
            tab_ref[hd, _T_CB] = jnp.exp(lgb * (row * 0.0 + c))

    def chunk_decay(hd, which):
        t = tab_ref[hd, which]
        return jnp.concatenate([t, t], axis=1)

    @pl.when(ph == 0)
    def _backward_states():
        @pl.when(i == 0)
        def _():
            sb_ref[...] = jnp.zeros_like(sb_ref)
        blk = nblk - 1 - i
        for cb in reversed(range(ncb)):
            r0 = cb * c
            for hd in range(RET_HEADS):
                k = qk_ref[0, r0:r0 + c, RET_Q_W + hd * RET_DK:RET_Q_W + (hd + 1) * RET_DK]
                v = v_ref[0, r0:r0 + c, hd * RET_DV:(hd + 1) * RET_DV]
                st = sb_ref[hd]
                sball_ref[blk * ncb + cb, hd] = st.astype(BF16)
                kd = (k.astype(F32) * tab_ref[hd, _T_KB]).astype(BF16)
                sb_ref[hd] = st * chunk_decay(hd, _T_CB) + _dot_tn(kd, v)

    @pl.when(ph == 1)
    def _forward():
        @pl.when(i == 0)
        def _():
            sf_ref[...] = jnp.zeros_like(sf_ref)
        for cb in range(ncb):
            r0 = cb * c
            for hd in range(RET_HEADS):
                q = qk_ref[0, r0:r0 + c, hd * RET_DK:(hd + 1) * RET_DK]
                k = qk_ref[0, r0:r0 + c, RET_Q_W + hd * RET_DK:RET_Q_W + (hd + 1) * RET_DK]
                v = v_ref[0, r0:r0 + c, hd * RET_DV:(hd + 1) * RET_DV]
                qf32 = q.astype(F32)
                kf32 = k.astype(F32)
                sd = (_dot_nt(q, k) * tab_ref[hd, _T_DMAT]).astype(BF16)
                st = sf_ref[hd]
                out = _dot(sd, v)
                out = out + _dot((qf32 * tab_ref[hd, _T_QF]).astype(BF16), st.astype(BF16))
                out = out + _dot((qf32 * tab_ref[hd, _T_QB]).astype(BF16), sball_ref[i * ncb + cb, hd])
                kd = (kf32 * tab_ref[hd, _T_KF]).astype(BF16)
                sf_ref[hd] = st * chunk_decay(hd, _T_CF) + _dot_tn(kd, v)
                mu = jnp.mean(out, axis=-1, keepdims=True)
                xc = out - mu
                var = jnp.mean(xc * xc, axis=-1, keepdims=True)
                y = xc * lax.rsqrt(var + EPS) * gn_ref[:, hd * RET_DV:(hd + 1) * RET_DV]
                gate = rg_ref[0, r0:r0 + c, hd * RET_DV:(hd + 1) * RET_DV].astype(F32)
                o_ref[0, r0:r0 + c, hd * RET_DV:(hd + 1) * RET_DV] = (y * gate).astype(BF16)


def _retention(dec, gn_g, qk, v, rg, ts):
    bsz, seq, _ = qk.shape
    nblk = seq // ts
    nchunk = seq // RET_CHUNK
    sweep = lambda b, ph, i: (b, i * ph + (nblk - 1 - i) * (1 - ph), 0)
    fwd_only = lambda b, ph, i: (b, i * ph, 0)
    return pl.pallas_call(
        functools.partial(_retention_kernel, ts=ts, nblk=nblk),
        grid=(bsz, 2, nblk),
        in_specs=[_resident((2, RET_HEADS, LANES)), _resident((1, RET_V_W)),
                  pl.BlockSpec((1, ts, 2 * RET_Q_W), sweep),
                  pl.BlockSpec((1, ts, RET_V_W), sweep),
                  pl.BlockSpec((1, ts, RET_V_W), fwd_only)],
        out_specs=pl.BlockSpec((1, ts, RET_V_W), fwd_only),
        out_shape=jax.ShapeDtypeStruct((bsz, seq, RET_V_W), BF16),
        scratch_shapes=[pltpu.VMEM((RET_HEADS, 7, RET_CHUNK, RET_CHUNK), F32),
                        pltpu.VMEM((RET_HEADS, RET_DK, RET_DV), F32),
                        pltpu.VMEM((RET_HEADS, RET_DK, RET_DV), F32),
                        pltpu.VMEM((nchunk, RET_HEADS, RET_DK, RET_DV), BF16)],
        compiler_params=_params("arbitrary", "arbitrary", "arbitrary"),
        name="retention",
    )(dec, gn_g, qk, v, rg)


def _rope64(x, cos, sin_lo, sin_hi):
    return x * cos + pltpu.roll(x, LANES - MLA_ROPE // 2, 1) * sin_lo + pltpu.roll(x, MLA_ROPE // 2, 1) * sin_hi


def _mla_prep_kernel(lat_ref, gcq_ref, wuq_ref, gckv_ref, wukv_ref, gqn_ref, gkn_ref,
                     cos_ref, slo_ref, shi_ref, qt_ref, k_ref, vt_ref):
    ts = lat_ref.shape[1]
    cos = cos_ref[...]
    slo = slo_ref[...]
    shi = shi_ref[...]

    def rms(x, g):
        ms = jnp.mean(x * x, axis=-1, keepdims=True)
        return x * lax.rsqrt(ms + EPS) * g

    cq = lat_ref[0, :, :Q_LORA].astype(F32)
    ckv = lat_ref[0, :, Q_LORA:Q_LORA + KV_LORA].astype(F32)
    kr = lat_ref[0, :, Q_LORA + KV_LORA:].astype(F32)
    cqn = rms(cq, gcq_ref[...]).astype(BF16)
    ckvn = rms(ckv, gckv_ref[...]).astype(BF16)
    gqn = gqn_ref[...]
    gkn = gkn_ref[...]
    kr = _rope64(kr, cos, slo, shi)
    kr_sq = jnp.sum(kr * kr, axis=-1, keepdims=True)
    kr_rot = _rope64(kr * gkn[:, MLA_NOPE:], cos, slo, shi)
    qscale = LOG2_E * MLA_QK ** -0.5
    for hd in range(MLA_HEADS):
        c0 = hd * MLA_QK_PAD
        qh = _dot(cqn, wuq_ref[:, c0:c0 + MLA_QK_PAD])
        ssq = jnp.sum(qh * qh, axis=-1, keepdims=True) * (1.0 / MLA_QK)
        qn = qh * lax.rsqrt(ssq + EPS) * gqn
        qh = jnp.concatenate([qn[:, :MLA_NOPE], _rope64(qn[:, MLA_NOPE:], cos, slo, shi)], axis=1) * qscale
        qt_ref[0, hd] = qh.T.astype(BF16)
        kvh = _dot(ckvn, wukv_ref[:, c0:c0 + MLA_QK_PAD])
        kn = kvh[:, :MLA_NOPE]
        ssk = (jnp.sum(kn * kn, axis=-1, keepdims=True) + kr_sq) * (1.0 / MLA_QK)
        rk = lax.rsqrt(ssk + EPS)
        k_ref[0, hd, :, :MLA_NOPE] = (kn * rk * gkn[:, :MLA_NOPE]).astype(BF16)
        k_ref[0, hd, :, MLA_NOPE:] = (kr_rot * rk).astype(BF16)
        for c in range(ts // LANES):
            vt_ref[0, hd, c] = kvh[c * LANES:(c + 1) * LANES, MLA_NOPE:].T.astype(BF16)


def _mla_prep(lat, g_cq, w_uq_p, g_ckv, w_ukv, g_qn_p, g_kn_p, cos_m, slo_m, shi_m, ts):
    bsz, seq, _ = lat.shape
    pos = pl.BlockSpec((ts, LANES), lambda b, i: (i, 0))
    head_blk = lambda w: pl.BlockSpec((1, MLA_HEADS, ts, w), lambda b, i: (b, 0, i, 0))
    return pl.pallas_call(
        _mla_prep_kernel,
        grid=(bsz, seq // ts),
        in_specs=[pl.BlockSpec((1, ts, LATENT_W), lambda b, i: (b, i, 0)),
                  _resident((1, Q_LORA)), _resident((Q_LORA, MLA_HEADS * MLA_QK_PAD)),
                  _resident((1, KV_LORA)), _resident((KV_LORA, MLA_HEADS * MLA_QK_PAD)),
                  _resident((1, MLA_QK_PAD)), _resident((1, MLA_QK_PAD)),
                  pos, pos, pos],
        out_specs=[pl.BlockSpec((1, MLA_HEADS, MLA_QK_PAD, ts), lambda b, i: (b, 0, 0, i)),
                   head_blk(MLA_QK_PAD),
                   pl.BlockSpec((1, MLA_HEADS, ts // LANES, MLA_V, LANES), lambda b, i: (b, 0, i, 0, 0))],
        out_shape=[jax.ShapeDtypeStruct((bsz, MLA_HEADS, MLA_QK_PAD, seq), BF16),
                   jax.ShapeDtypeStruct((bsz, MLA_HEADS, seq, MLA_QK_PAD), BF16),
                   jax.ShapeDtypeStruct((bsz, MLA_HEADS, seq // LANES, MLA_V, LANES), BF16)],
        compiler_params=_params("arbitrary", "arbitrary"),
        name="mla_prep",
    )(lat, g_cq, w_uq_p, g_ckv, w_ukv, g_qn_p, g_kn_p, cos_m, slo_m, shi_m)


ATT_UNROLL = 4
ATT_PITCH_PAD = LANES
ATT_FIXED_SHIFT_MAX = 60.0


def _attention_kernel(qt_ref, k_ref, vt_ref, o_ref, s0_ref, s1_ref, kmax_ref, *, tk, nkv):
    qi = pl.program_id(2)
    qt = qt_ref[0, 0]
    tq = qt.shape[1]
    sub = tk // LANES
    s_refs = (s0_ref, s1_ref)

    def key_block(j):
        k0 = j * tk
        if not isinstance(j, int):
            k0 = pl.multiple_of(k0, tk)
        return k_ref[0, 0, pl.ds(k0, tk), :]

    def scores(j):
        return _dot(key_block(j), qt)

    def values_t(j):
        vt3 = vt_ref[0, 0, pl.ds(j * sub, sub)]
        return jnp.concatenate([vt3[i] for i in range(sub)], axis=1)

    def blocks(n_trips, trip_fn, tail_fn, carry):
        carry = lax.fori_loop(0, n_trips, trip_fn, carry)
        for j in range(n_trips * ATT_UNROLL, nkv):
            carry = tail_fn(j, carry)
        return carry

    @pl.when(qi == 0)
    def _largest_key_norm():
        ones = jnp.ones((MLA_QK_PAD, LANES), BF16)

        def body(j, mx):
            kc = key_block(j).astype(F32)
            return jnp.maximum(mx, _dot((kc * kc).astype(BF16), ones))

        mx = lax.fori_loop(0, nkv, body, jnp.zeros((tk, LANES), F32))
        kmax_ref[...] = jnp.max(mx, axis=0, keepdims=True)

    q32 = qt.astype(F32)
    qn2 = jnp.sum(q32 * q32, axis=0, keepdims=True)
    kn2 = jnp.concatenate([kmax_ref[...]] * (tq // LANES), axis=1)
    shift = jnp.sqrt(qn2 * kn2)
    fixed_ok = jnp.max(shift) <= ATT_FIXED_SHIFT_MAX

    @pl.when(fixed_ok)
    def _fixed_shift():
        def block(j, carry):
            l, acc = carry
            e = jnp.exp2(scores(j) - shift)
            return l + jnp.sum(e, axis=0, keepdims=True), acc + _dot(values_t(j), e.astype(BF16))

        def trip(i, carry):
            for u in range(ATT_UNROLL):
                carry = block(i * ATT_UNROLL + u, carry)
            return carry

        carry = (jnp.zeros((1, tq), F32), jnp.zeros((MLA_V, tq), F32))
        l, acc = blocks(nkv // ATT_UNROLL, trip, block, carry)
        o_ref[0] = (acc / l).T.astype(BF16)

    @pl.when(jnp.logical_not(fixed_ok))
    def _running_max():
        def update(s_ref, j, carry):
            m, l, acc = carry
            s = s_ref[:, :tq]
            m_new = jnp.maximum(m, jnp.max(s, axis=0, keepdims=True))
            p = jnp.exp2(s - m_new)
            alpha = jnp.exp2(m - m_new)
            l = alpha * l + jnp.sum(p, axis=0, keepdims=True)
            acc = alpha * acc + _dot(values_t(j), p.astype(BF16))
            return m_new, l, acc

        def trip(i, carry):
            j = i * ATT_UNROLL
            for u in range(ATT_UNROLL):
                s_refs[(u + 1) % 2][:, :tq] = scores(j + u + 1)
                carry = update(s_refs[u % 2], j + u, carry)
            return carry

        def tail(j, carry):
            if j + 1 < nkv:
                s_refs[(j + 1) % 2][:, :tq] = scores(j + 1)
            return update(s_refs[j % 2], j, carry)

        carry = (jnp.full((1, tq), -jnp.inf, F32), jnp.zeros((1, tq), F32), jnp.zeros((MLA_V, tq), F32))
        s0_ref[:, :tq] = scores(0)
        _, l, acc = blocks((nkv - 1) // ATT_UNROLL, trip, tail, carry)
        o_ref[0] = (acc / l).T.astype(BF16)


def _attention(qt, k, vt, tq, tk):
    bsz, nh, seq, _ = k.shape
    nkv = seq // tk
    assert nkv >= 2 and nkv % 2 == 0 and tk % LANES == 0, (seq, tk)
    return pl.pallas_call(
        functools.partial(_attention_kernel, tk=tk, nkv=nkv),
        grid=(bsz, nh, seq // tq),
        in_specs=[pl.BlockSpec((1, 1, MLA_QK_PAD, tq), lambda b, h, i: (b, h, 0, i)),
                  pl.BlockSpec((1, 1, seq, MLA_QK_PAD), lambda b, h, i: (b, h, 0, 0)),
                  pl.BlockSpec((1, 1, seq // LANES, MLA_V, LANES), lambda b, h, i: (b, h, 0, 0, 0))],
        out_specs=pl.BlockSpec((1, tq, MLA_V), lambda b, h, i: (b, i, h)),
        out_shape=jax.ShapeDtypeStruct((bsz, seq, nh * MLA_V), BF16),
        scratch_shapes=[pltpu.VMEM((tk, tq + ATT_PITCH_PAD), F32), pltpu.VMEM((tk, tq + ATT_PITCH_PAD), F32),
                        pltpu.VMEM((1, LANES), F32)],
        compiler_params=_params("arbitrary", "arbitrary", "arbitrary"),
        name="attention",
    )(qt, k, vt)


def _merge_kernel(x_ref, ret_ref, mla_ref, gates_ref, wr_ref, wm_ref, wo_ref, o_ref):
    ret_branch = _dot(ret_ref[...], wr_ref[...])
    mla_branch = _dot(mla_ref[...], wm_ref[...])
    merged = (gates_ref[:, :D_MODEL].astype(F32) * ret_branch
              + gates_ref[:, D_MODEL:].astype(F32) * mla_branch)
    o_ref[...] = x_ref[...] + _dot(merged.astype(BF16), wo_ref[...])


def _merge(x2, ret, mla, gates, w_ret_o, w_mla_o, w_out, tm):
    t = x2.shape[0]
    row = lambda w: pl.BlockSpec((tm, w), lambda i: (i, 0))
    sq = _resident((D_MODEL, D_MODEL))
    return pl.pallas_call(
        _merge_kernel,
        grid=(t // tm,),
        in_specs=[row(D_MODEL), row(RET_V_W), row(D_MODEL), row(2 * D_MODEL), sq, sq, sq],
        out_specs=row(D_MODEL),
        out_shape=jax.ShapeDtypeStruct((t, D_MODEL), F32),
        compiler_params=_params("arbitrary"),
        name="merge",
    )(x2, ret, mla, gates, w_ret_o, w_mla_o, w_out)


FFN_HALO = 8
FFN_NC = 256
FFN_U_BUFS = 4


def _ffn_kernel(x_ref, prev_ref, next_ref, g_ref, wup_ref, cw_ref, cb_ref, wdn_ref, o_ref, h_ref, u_ref,
                act_ref, *, ts, nblk):
    i = pl.program_id(1)
    g = g_ref[...]

    def rms(x):
        ms = jnp.mean(x * x, axis=-1, keepdims=True)
        return x * lax.rsqrt(ms + EPS) * g

    keep_prev = jnp.where(i > 0, 1.0, 0.0)
    keep_next = jnp.where(i < nblk - 1, 1.0, 0.0)
    xm = x_ref[0]
    h_ref[:FFN_HALO] = (rms(prev_ref[0]) * keep_prev).astype(BF16)
    h_ref[FFN_HALO:FFN_HALO + ts] = rms(xm).astype(BF16)
    h_ref[FFN_HALO + ts:] = (rms(next_ref[0]) * keep_next).astype(BF16)
    h = h_ref[...]

    def conv(slot, c0):
        u_ref[slot] = _dot(h, wup_ref[:, c0:c0 + FFN_NC])
        w = cw_ref[:, c0:c0 + FFN_NC]
        return (u_ref[slot, pl.ds(FFN_HALO - 1, ts), :] * w[0:1]
                + u_ref[slot, pl.ds(FFN_HALO, ts), :] * w[1:2]
                + u_ref[slot, pl.ds(FFN_HALO + 1, ts), :] * w[2:3]
                + cb_ref[:, c0:c0 + FFN_NC])

    for n in range(D_FF // FFN_NC):
        ca = n * FFN_NC
        ua = conv((2 * n) % FFN_U_BUFS, ca)
        ub = conv((2 * n + 1) % FFN_U_BUFS, D_FF + ca)
        act_ref[:, ca:ca + FFN_NC] = (ua * _sigmoid(ua) * ub).astype(BF16)
    o_ref[0] = xm + _dot(act_ref[...], wdn_ref[...])


def _ffn(x1, g_ffn, w_up, conv_w, conv_b, w_down, ts):
    bsz, seq, _ = x1.shape
    nblk = seq // ts
    hb = ts // FFN_HALO
    last_halo = seq // FFN_HALO - 1
    return pl.pallas_call(
        functools.partial(_ffn_kernel, ts=ts, nblk=nblk),
        grid=(bsz, nblk),
        in_specs=[pl.BlockSpec((1, ts, D_MODEL), lambda b, i: (b, i, 0)),
                  pl.BlockSpec((1, FFN_HALO, D_MODEL), lambda b, i: (b, jnp.maximum(i * hb - 1, 0), 0)),
                  pl.BlockSpec((1, FFN_HALO, D_MODEL), lambda b, i: (b, jnp.minimum((i + 1) * hb, last_halo), 0)),
                  _resident((1, D_MODEL)), _resident((D_MODEL, 2 * D_FF)),
                  _resident((3, 2 * D_FF)), _resident((1, 2 * D_FF)), _resident((D_FF, D_MODEL))],
        out_specs=pl.BlockSpec((1, ts, D_MODEL), lambda b, i: (b, i, 0)),
        out_shape=jax.ShapeDtypeStruct((bsz, seq, D_MODEL), F32),
        scratch_shapes=[pltpu.VMEM((ts + 2 * FFN_HALO, D_MODEL), BF16),
                        pltpu.VMEM((FFN_U_BUFS, ts + 2 * FFN_HALO, FFN_NC), F32),
                        pltpu.VMEM((ts, D_FF), BF16)],
        compiler_params=_params("arbitrary", "arbitrary"),
        name="ffn",
    )(x1, x1, x1, g_ffn, w_up, conv_w, conv_b, w_down)


def _rope_tables(seq):
    pos = jnp.arange(seq).astype(F32)[:, None]

    def cos_sin(d):
        inv = ROPE_BASE ** (-jnp.arange(0, d, 2, dtype=F32) / d)
        ang = pos * inv[None, :]
        return jnp.cos(ang), jnp.sin(ang)

    cr, sr = cos_sin(RET_DK)
    cm, sm = cos_sin(MLA_ROPE)
    zh = jnp.zeros_like(sm)
    zpad = jnp.zeros((seq, LANES - MLA_ROPE), F32)
    ret = (jnp.concatenate([cr, cr], 1), jnp.concatenate([-sr, sr], 1))
    mla = (jnp.concatenate([cm, cm, zpad], 1), jnp.concatenate([-sm, zh, zpad], 1),
           jnp.concatenate([zh, sm, zpad], 1))
    return ret, mla


def _tile(seq, want):
    t = min(seq, want)
    assert seq % t == 0, (seq, t)
    return t


def _trunk(x, p, depth):
    bsz, seq, _ = x.shape
    t = bsz * seq
    (cos_r, sin_r), (cos_m, slo_m, shi_m) = _rope_tables(seq)
    tm = _tile(seq, 512)
    for l in range(depth):
        w = {k: v[l] for k, v in p.items()}
        x2 = x.reshape(t, D_MODEL)
        qk, rv, rg, lat, gates = _inproj(x2, w["g_mix"], w["w_in"], cos_r, sin_r, seq, tm)
        ret = _retention(w["dec"], w["ret_gn_g"],
                         qk.reshape(bsz, seq, -1), rv.reshape(bsz, seq, -1), rg.reshape(bsz, seq, -1),
                         _tile(seq, 512))
        q, k, v = _mla_prep(lat.reshape(bsz, seq, -1), w["g_cq"], w["w_uq"], w["g_ckv"], w["w_ukv"],
                            w["g_qn"], w["g_kn"], cos_m, slo_m, shi_m, _tile(seq, 512))
        mla = _attention(q, k, v, _tile(seq, 1024), _tile(seq, 512))
        x1 = _merge(x2, ret.reshape(t, -1), mla.reshape(t, -1), gates,
                    w["w_ret_o"], w["w_mla_o"], w["w_out"], tm)
        x = _ffn(x1.reshape(bsz, seq, D_MODEL), w["g_ffn"], w["w_up"], w["conv_w"], w["conv_b"],
                 w["w_down"], _tile(seq, 512))
    return x


def _prepare_weights(g_mix, w_in, ret_decay_fwd, ret_decay_bwd, ret_gn_g, w_ret_o, g_cq, w_uq, g_ckv,
                     w_ukv, g_qn, g_kn, w_mla_o, w_out, g_ffn, w_up, conv_w, conv_b, w_down):
    depth = w_in.shape[0]
    kr_end = 2 * RET_Q_W + 2 * RET_V_W + Q_LORA + KV_LORA + MLA_ROPE
    w_in_p = jnp.concatenate(
        [w_in[:, :, :kr_end], jnp.zeros((depth, D_MODEL, LANES - MLA_ROPE), w_in.dtype), w_in[:, :, kr_end:]],
        axis=2).astype(BF16)
    pad_h = MLA_QK_PAD - MLA_QK
    w_uq_p = jnp.pad(w_uq.reshape(depth, Q_LORA, MLA_HEADS, MLA_QK), ((0, 0), (0, 0), (0, 0), (0, pad_h)))
    w_uq_p = w_uq_p.reshape(depth, Q_LORA, MLA_HEADS * MLA_QK_PAD).astype(BF16)
    dec = jnp.stack([ret_decay_fwd, ret_decay_bwd], axis=1).astype(F32)
    dec = jnp.broadcast_to(dec[..., None], (depth, 2, RET_HEADS, LANES))
    row = lambda a: a[:, None, :].astype(F32)
    return {
        "g_mix": row(g_mix), "w_in": w_in_p, "dec": dec, "ret_gn_g": row(ret_gn_g),
        "w_ret_o": w_ret_o.astype(BF16), "g_cq": row(g_cq), "w_uq": w_uq_p, "g_ckv": row(g_ckv),
        "w_ukv": w_ukv.astype(BF16),
        "g_qn": row(jnp.pad(g_qn, ((0, 0), (0, pad_h)))), "g_kn": row(jnp.pad(g_kn, ((0, 0), (0, pad_h)))),
        "w_mla_o": w_mla_o.astype(BF16), "w_out": w_out.astype(BF16), "g_ffn": row(g_ffn),
        "w_up": w_up.astype(BF16), "conv_w": conv_w.astype(F32), "conv_b": row(conv_b),
        "w_down": w_down.astype(BF16),
    }


def kernel(x_prompt, x_sample, g_mix, w_in, ret_decay_fwd, ret_decay_bwd, ret_gn_g, w_ret_o, g_cq, w_uq,
           g_ckv, w_ukv, g_qn, g_kn, w_mla_o, w_out, g_ffn, w_up, conv_w, conv_b, w_down):
    depth = w_in.shape[0]
    p = _prepare_weights(g_mix, w_in, ret_decay_fwd, ret_decay_bwd, ret_gn_g, w_ret_o, g_cq, w_uq, g_ckv,
                         w_ukv, g_qn, g_kn, w_mla_o, w_out, g_ffn, w_up, conv_w, conv_b, w_down)
    return (_trunk(x_prompt, p, depth), _trunk(x_sample, p, depth))
```

```python
import functools

import jax
import jax.numpy as jnp
from jax import lax
from jax.experimental import pallas as pl
from jax.experimental.pallas import tpu as pltpu

D_MODEL = 1024
RET_HEADS = 4
RET_DK = 128
RET_DV = 256
RET_CHUNK = 128
MLA_HEADS = 8
MLA_NOPE = 128
MLA_ROPE = 64
MLA_V = 128
MLA_QK = MLA_NOPE + MLA_ROPE
MLA_QK_PAD = 256
Q_LORA = 384
KV_LORA = 256
D_FF = 2816
ROPE_BASE = 10000.0
EPS = 1e-6
LOG2_E = 1.4426950408889634

RET_Q_W = RET_HEADS * RET_DK
RET_V_W = RET_HEADS * RET_DV
LATENT_W = Q_LORA + KV_LORA + 128
IN_W_PAD = 2 * RET_Q_W + 2 * RET_V_W + LATENT_W + 2 * D_MODEL

LANES = 128
VMEM_LIMIT_BYTES = 56 * 1024 * 1024

F32 = jnp.float32
BF16 = jnp.bfloat16


def _params(*semantics):
    return pltpu.CompilerParams(dimension_semantics=semantics, vmem_limit_bytes=VMEM_LIMIT_BYTES)


def _resident(shape):
    return pl.BlockSpec(shape, lambda *_: (0,) * len(shape), pipeline_mode=pl.Buffered(1))


def _sigmoid(x):
    return 1.0 / (1.0 + jnp.exp(-x))


def _dot(a, b):
    return jnp.dot(a, b, preferred_element_type=F32)


def _dot_nt(a, b):
    return lax.dot_general(a, b, (((1,), (1,)), ((), ())), preferred_element_type=F32)


def _dot_tn(a, b):
    return lax.dot_general(a, b, (((0,), (0,)), ((), ())), preferred_element_type=F32)


def _inproj_kernel(x_ref, g_ref, w_ref, cos_ref, sin_ref,
                   qk_ref, v_ref, rg_ref, lat_ref, gates_ref):
    xf = x_ref[...]
    ms = jnp.mean(xf * xf, axis=-1, keepdims=True)
    h = (xf * lax.rsqrt(ms + EPS) * g_ref[...]).astype(BF16)
    cos = cos_ref[...]
    sin = sin_ref[...]

    def proj(c0, width):
        return _dot(h, w_ref[:, c0:c0 + width])

    for seg, scale in ((0, None), (1, RET_DK ** -0.5)):
        acc = proj(seg * RET_Q_W, RET_Q_W)
        for hd in range(RET_HEADS):
            xs = acc[:, hd * RET_DK:(hd + 1) * RET_DK]
            r = xs * cos + pltpu.roll(xs, RET_DK // 2, 1) * sin
            if scale is not None:
                r = r * scale
            c0 = seg * RET_Q_W + hd * RET_DK
            qk_ref[:, c0:c0 + RET_DK] = r.astype(BF16)
    base = 2 * RET_Q_W
    half = RET_V_W // 2
    for j in range(2):
        v_ref[:, j * half:(j + 1) * half] = proj(base + j * half, half).astype(BF16)
    base += RET_V_W
    for j in range(2):
        a = proj(base + j * half, half)
        rg_ref[:, j * half:(j + 1) * half] = (a * _sigmoid(a)).astype(BF16)
    base += RET_V_W
    lat_ref[...] = proj(base, LATENT_W).astype(BF16)
    base += LATENT_W
    for j in range(4):
        a = proj(base + j * half, half)
        gates_ref[:, j * half:(j + 1) * half] = _sigmoid(a).astype(BF16)


def _inproj(x2, g_mix, w_in_p, cos_r, sin_r, seq, tm):
    t = x2.shape[0]
    nseq = seq // tm
    row = lambda w: pl.BlockSpec((tm, w), lambda i: (i, 0))
    pos = pl.BlockSpec((tm, LANES), lambda i: (i % nseq, 0))
    out_w = (2 * RET_Q_W, RET_V_W, RET_V_W, LATENT_W, 2 * D_MODEL)
    return pl.pallas_call(
        _inproj_kernel,
        grid=(t // tm,),
        in_specs=[row(D_MODEL), _resident((1, D_MODEL)), _resident((D_MODEL, IN_W_PAD)), pos, pos],
        out_specs=[row(w) for w in out_w],
        out_shape=[jax.ShapeDtypeStruct((t, w), BF16) for w in out_w],
        compiler_params=_params("arbitrary"),
        name="inproj",
    )(x2, g_mix, w_in_p, cos_r, sin_r)


_T_DMAT, _T_QF, _T_QB, _T_KF, _T_KB, _T_CF, _T_CB = range(7)


def _retention_kernel(dec_ref, gn_ref, qk_ref, v_ref, rg_ref, o_ref,
                      tab_ref, sf_ref, sb_ref, sball_ref, *, ts, nblk):
    c = RET_CHUNK
    ncb = ts // c
    b = pl.program_id(0)
    ph = pl.program_id(1)
    i = pl.program_id(2)

    @pl.when((b == 0) & (ph == 0) & (i == 0))
    def _tables():
        row = lax.broadcasted_iota(jnp.int32, (c, c), 0).astype(F32)
        col = lax.broadcasted_iota(jnp.int32, (c, c), 1).astype(F32)
        diff = row - col
        for hd in range(RET_HEADS):
            def log_sigmoid(d):
                return jnp.minimum(d, 0.0) - jnp.log1p(jnp.exp(-jnp.abs(d)))
            lgf = log_sigmoid(dec_ref[0, hd:hd + 1, :])
            lgb = log_sigmoid(dec_ref[1, hd:hd + 1, :])
            tab_ref[hd, _T_DMAT] = jnp.where(diff >= 0.0,
                                             jnp.exp(lgf * jnp.maximum(diff, 0.0)),
                                             jnp.exp(lgb * jnp.maximum(-diff, 0.0)))
            tab_ref[hd, _T_QF] = jnp.exp(lgf * (row + 1.0))
            tab_ref[hd, _T_QB] = jnp.exp(lgb * (c - row))
            tab_ref[hd, _T_KF] = jnp.exp(lgf * (c - 1.0 - row))
            tab_ref[hd, _T_KB] = jnp.exp(lgb * row)
            tab_ref[hd, _T_CF] = jnp.exp(lgf * (row * 0.0 + c))
            tab_ref[hd, _T_CB] = jnp.exp(lgb * (row * 0.0 + c))

    def chunk_decay(hd, which):
        t = tab_ref[hd, which]
        return jnp.concatenate([t, t], axis=1)

    @pl.when(ph == 0)
    def _backward_states():
        @pl.when(i == 0)
        def _():
            sb_ref[...] = jnp.zeros_like(sb_ref)
        blk = nblk - 1 - i
        for cb in reversed(range(ncb)):
            r0 = cb * c
            for hd in range(RET_HEADS):
                k = qk_ref[0, r0:r0 + c, RET_Q_W + hd * RET_DK:RET_Q_W + (hd + 1) * RET_DK]
                v = v_ref[0, r0:r0 + c, hd * RET_DV:(hd + 1) * RET_DV]
                st = sb_ref[hd]
                sball_ref[blk * ncb + cb, hd] = st.astype(BF16)
                kd = (k.astype(F32) * tab_ref[hd, _T_KB]).astype(BF16)
                sb_ref[hd] = st * chunk_decay(hd, _T_CB) + _dot_tn(kd, v)

    @pl.when(ph == 1)
    def _forward():
        @pl.when(i == 0)
        def _():
            sf_ref[...] = jnp.zeros_like(sf_ref)
        for cb in range(ncb):
            r0 = cb * c
            for hd in range(RET_HEADS):
                q = qk_ref[0, r0:r0 + c, hd * RET_DK:(hd + 1) * RET_DK]
                k = qk_ref[0, r0:r0 + c, RET_Q_W + hd * RET_DK:RET_Q_W + (hd + 1) * RET_DK]
                v = v_ref[0, r0:r0 + c, hd * RET_DV:(hd + 1) * RET_DV]
                qf32 = q.astype(F32)
                kf32 = k.astype(F32)
                sd = (_dot_nt(q, k) * tab_ref[hd, _T_DMAT]).astype(BF16)
                st = sf_ref[hd]
                out = _dot(sd, v)
                out = out + _dot((qf32 * tab_ref[hd, _T_QF]).astype(BF16), st.astype(BF16))
                out = out + _dot((qf32 * tab_ref[hd, _T_QB]).astype(BF16), sball_ref[i * ncb + cb, hd])
                kd = (kf32 * tab_ref[hd, _T_KF]).astype(BF16)
                sf_ref[hd] = st * chunk_decay(hd, _T_CF) + _dot_tn(kd, v)
                mu = jnp.mean(out, axis=-1, keepdims=True)
                xc = out - mu
                var = jnp.mean(xc * xc, axis=-1, keepdims=True)
                y = xc * lax.rsqrt(var + EPS) * gn_ref[:, hd * RET_DV:(hd + 1) * RET_DV]
                gate = rg_ref[0, r0:r0 + c, hd * RET_DV:(hd + 1) * RET_DV].astype(F32)
                o_ref[0, r0:r0 + c, hd * RET_DV:(hd + 1) * RET_DV] = (y * gate).astype(BF16)


def _retention(dec, gn_g, qk, v, rg, ts):
    bsz, seq, _ = qk.shape
    nblk = seq // ts
    nchunk = seq // RET_CHUNK
    sweep = lambda b, ph, i: (b, i * ph + (nblk - 1 - i) * (1 - ph), 0)
    fwd_only = lambda b, ph, i: (b, i * ph, 0)
    return pl.pallas_call(
        functools.partial(_retention_kernel, ts=ts, nblk=nblk),
        grid=(bsz, 2, nblk),
        in_specs=[_resident((2, RET_HEADS, LANES)), _resident((1, RET_V_W)),
                  pl.BlockSpec((1, ts, 2 * RET_Q_W), sweep),
                  pl.BlockSpec((1, ts, RET_V_W), sweep),
                  pl.BlockSpec((1, ts, RET_V_W), fwd_only)],
        out_specs=pl.BlockSpec((1, ts, RET_V_W), fwd_only),
        out_shape=jax.ShapeDtypeStruct((bsz, seq, RET_V_W), BF16),
        scratch_shapes=[pltpu.VMEM((RET_HEADS, 7, RET_CHUNK, RET_CHUNK), F32),
                        pltpu.VMEM((RET_HEADS, RET_DK, RET_DV), F32),
                        pltpu.VMEM((RET_HEADS, RET_DK, RET_DV), F32),
                        pltpu.VMEM((nchunk, RET_HEADS, RET_DK, RET_DV), BF16)],
        compiler_params=_params("arbitrary", "arbitrary", "arbitrary"),
        name="retention",
    )(dec, gn_g, qk, v, rg)


def _rope64(x, cos, sin_lo, sin_hi):
    return x * cos + pltpu.roll(x, LANES - MLA_ROPE // 2, 1) * sin_lo + pltpu.roll(x, MLA_ROPE // 2, 1) * sin_hi


def _mla_prep_kernel(lat_ref, gcq_ref, wuqt_ref, gckv_ref, wk_ref, wvt_ref, gqn_ref, gkn_ref,
                     cos_ref, slo_ref, shi_ref, cost_ref, sint_ref, qt_ref, k_ref, vt_ref):
    ts = lat_ref.shape[1]

    def rms(x, g):
        ms = jnp.mean(x * x, axis=-1, keepdims=True)
        return x * lax.rsqrt(ms + EPS) * g

    cqn = rms(lat_ref[0, :, :Q_LORA].astype(F32), gcq_ref[...])
    ckvn = rms(lat_ref[0, :, Q_LORA:Q_LORA + KV_LORA].astype(F32), gckv_ref[...])
    kr = lat_ref[0, :, Q_LORA + KV_LORA:].astype(F32)
    cqn_t = cqn.T.astype(BF16)
    ckvn_t = ckvn.T.astype(BF16)

    vt_all = _dot(wvt_ref[...], ckvn_t)
    for hd in range(MLA_HEADS):
        for c in range(ts // LANES):
            vt_ref[0, hd, c] = vt_all[hd * MLA_V:(hd + 1) * MLA_V, c * LANES:(c + 1) * LANES].astype(BF16)

    half = MLA_ROPE // 2
    r0 = MLA_NOPE
    for hd in range(MLA_HEADS):
        qh = _dot(wuqt_ref[hd * MLA_QK_PAD:(hd + 1) * MLA_QK_PAD, :], cqn_t)
        for c in range(ts // LANES):
            lanes = slice(c * LANES, (c + 1) * LANES)
            qc = qh[:, lanes]
            ssq = jnp.sum(qc * qc, axis=0, keepdims=True) * (1.0 / MLA_QK)
            qn = qc * lax.rsqrt(ssq + EPS) * gqn_ref[...]
            x1 = qn[r0:r0 + half]
            x2 = qn[r0 + half:r0 + 2 * half]
            cos = cost_ref[:, lanes]
            sin = sint_ref[:, lanes]
            qt_ref[0, hd, :r0, lanes] = qn[:r0].astype(BF16)
            qt_ref[0, hd, r0:r0 + half, lanes] = (x1 * cos - x2 * sin).astype(BF16)
            qt_ref[0, hd, r0 + half:r0 + 2 * half, lanes] = (x2 * cos + x1 * sin).astype(BF16)
            qt_ref[0, hd, r0 + 2 * half:, lanes] = jnp.zeros((MLA_QK_PAD - MLA_QK, LANES), BF16)

    cos = cos_ref[...]
    slo = slo_ref[...]
    shi = shi_ref[...]
    gkn = gkn_ref[...]
    kr = _rope64(kr, cos, slo, shi)
    kr_sq = jnp.sum(kr * kr, axis=-1, keepdims=True)
    kr_rot = _rope64(kr * gkn[:, MLA_NOPE:], cos, slo, shi)
    kn_all = _dot(ckvn.astype(BF16), wk_ref[...])
    for hd in range(MLA_HEADS):
        kn = kn_all[:, hd * MLA_NOPE:(hd + 1) * MLA_NOPE]
        ssk = (jnp.sum(kn * kn, axis=-1, keepdims=True) + kr_sq) * (1.0 / MLA_QK)
        rk = lax.rsqrt(ssk + EPS)
        k_ref[0, hd, :, :MLA_NOPE] = (kn * rk * gkn[:, :MLA_NOPE]).astype(BF16)
        k_ref[0, hd, :, MLA_NOPE:] = (kr_rot * rk).astype(BF16)


def _mla_prep(lat, w, tabs, ts):
    bsz, seq, _ = lat.shape
    cos_m, slo_m, shi_m, cos_t, sin_t = tabs
    pos = pl.BlockSpec((ts, LANES), lambda b, i: (i, 0))
    pos_t = pl.BlockSpec((MLA_ROPE // 2, ts), lambda b, i: (0, i))
    return pl.pallas_call(
        _mla_prep_kernel,
        grid=(bsz, seq // ts),
        in_specs=[pl.BlockSpec((1, ts, LATENT_W), lambda b, i: (b, i, 0)),
                  _resident((1, Q_LORA)), _resident((MLA_HEADS * MLA_QK_PAD, Q_LORA)),
                  _resident((1, KV_LORA)), _resident((KV_LORA, MLA_HEADS * MLA_NOPE)),
                  _resident((MLA_HEADS * MLA_V, KV_LORA)),
                  _resident((MLA_QK_PAD, LANES)), _resident((1, MLA_QK_PAD)),
                  pos, pos, pos, pos_t, pos_t],
        out_specs=[pl.BlockSpec((1, MLA_HEADS, MLA_QK_PAD, ts), lambda b, i: (b, 0, 0, i)),
                   pl.BlockSpec((1, MLA_HEADS, ts, MLA_QK_PAD), lambda b, i: (b, 0, i, 0)),
                   pl.BlockSpec((1, MLA_HEADS, ts // LANES, MLA_V, LANES), lambda b, i: (b, 0, i, 0, 0))],
        out_shape=[jax.ShapeDtypeStruct((bsz, MLA_HEADS, MLA_QK_PAD, seq), BF16),
                   jax.ShapeDtypeStruct((bsz, MLA_HEADS, seq, MLA_QK_PAD), BF16),
                   jax.ShapeDtypeStruct((bsz, MLA_HEADS, seq // LANES, MLA_V, LANES), BF16)],
        compiler_params=_params("arbitrary", "arbitrary"),
        name="mla_prep",
    )(lat, w["g_cq"], w["w_uq_t"], w["g_ckv"], w["w_k"], w["w_v_t"], w["g_qn_rows"], w["g_kn"],
      cos_m, slo_m, shi_m, cos_t, sin_t)


ATT_UNROLL = 8
ATT_PITCH_PAD = LANES
ATT_FIXED_SHIFT_MAX = 60.0


def _attention_kernel(shift_ref, qt_ref, k_ref, vt_ref, o_ref, s0_ref, s1_ref, *, tk, nkv):
    qt = qt_ref[0, 0]
    tq = qt.shape[1]
    sub = tk // LANES
    s_refs = (s0_ref, s1_ref)

    def scores(j):
        k0 = j * tk
        if not isinstance(j, int):
            k0 = pl.multiple_of(k0, tk)
        return _dot(k_ref[0, 0, pl.ds(k0, tk), :], qt)

    def values_t(j):
        vt3 = vt_ref[0, 0, pl.ds(j * sub, sub)]
        return jnp.concatenate([vt3[i] for i in range(sub)], axis=1)

    def blocks(n_trips, trip_fn, tail_fn, carry):
        carry = lax.fori_loop(0, n_trips, trip_fn, carry)
        for j in range(n_trips * ATT_UNROLL, nkv):
            carry = tail_fn(j, carry)
        return carry

    shift = shift_ref[0, 0]
    fixed_ok = shift <= ATT_FIXED_SHIFT_MAX

    @pl.when(fixed_ok)
    def _fixed_shift():
        def block(j, carry):
            l, acc = carry
            e = jnp.exp2(scores(j) - shift)
            return l + jnp.sum(e, axis=0, keepdims=True), acc + _dot(values_t(j), e.astype(BF16))

        def trip(i, carry):
            for u in range(ATT_UNROLL):
                carry = block(i * ATT_UNROLL + u, carry)
            return carry

        carry = (jnp.zeros((1, tq), F32), jnp.zeros((MLA_V, tq), F32))
        l, acc = blocks(nkv // ATT_UNROLL, trip, block, carry)
        o_ref[0] = (acc / l).T.astype(BF16)

    @pl.when(jnp.logical_not(fixed_ok))
    def _running_max():
        def update(s_ref, j, carry):
            m, l, acc = carry
            s = s_ref[:, :tq]
            m_new = jnp.maximum(m, jnp.max(s, axis=0, keepdims=True))
            p = jnp.exp2(s - m_new)
            alpha = jnp.exp2(m - m_new)
            l = alpha * l + jnp.sum(p, axis=0, keepdims=True)
            acc = alpha * acc + _dot(values_t(j), p.astype(BF16))
            return m_new, l, acc

        def trip(i, carry):
            j = i * ATT_UNROLL
            for u in range(ATT_UNROLL):
                s_refs[(u + 1) % 2][:, :tq] = scores(j + u + 1)
                carry = update(s_refs[u % 2], j + u, carry)
            return carry

        def tail(j, carry):
            if j + 1 < nkv:
                s_refs[(j + 1) % 2][:, :tq] = scores(j + 1)
            return update(s_refs[j % 2], j, carry)

        carry = (jnp.full((1, tq), -jnp.inf, F32), jnp.zeros((1, tq), F32), jnp.zeros((MLA_V, tq), F32))
        s0_ref[:, :tq] = scores(0)
        _, l, acc = blocks((nkv - 1) // ATT_UNROLL, trip, tail, carry)
        o_ref[0] = (acc / l).T.astype(BF16)


def _attention(shift, qt, k, vt, tq, tk):
    bsz, nh, seq, _ = k.shape
    nkv = seq // tk
    assert nkv >= 2 and nkv % 2 == 0 and tk % LANES == 0, (seq, tk)
    return pl.pallas_call(
        functools.partial(_attention_kernel, tk=tk, nkv=nkv),
        grid=(bsz, nh, seq // tq),
        in_specs=[pl.BlockSpec(memory_space=pltpu.SMEM),
                  pl.BlockSpec((1, 1, MLA_QK_PAD, tq), lambda b, h, i: (b, h, 0, i)),
                  pl.BlockSpec((1, 1, seq, MLA_QK_PAD), lambda b, h, i: (b, h, 0, 0)),
                  pl.BlockSpec((1, 1, seq // LANES, MLA_V, LANES), lambda b, h, i: (b, h, 0, 0, 0))],
        out_specs=pl.BlockSpec((1, tq, MLA_V), lambda b, h, i: (b, i, h)),
        out_shape=jax.ShapeDtypeStruct((bsz, seq, nh * MLA_V), BF16),
        scratch_shapes=[pltpu.VMEM((tk, tq + ATT_PITCH_PAD), F32), pltpu.VMEM((tk, tq + ATT_PITCH_PAD), F32)],
        compiler_params=_params("arbitrary", "arbitrary", "arbitrary"),
        name="attention",
    )(shift, qt, k, vt)


def _merge_kernel(x_ref, ret_ref, mla_ref, gates_ref, wr_ref, wm_ref, wo_ref, o_ref):
    ret_branch = _dot(ret_ref[...], wr_ref[...])
    mla_branch = _dot(mla_ref[...], wm_ref[...])
    merged = (gates_ref[:, :D_MODEL].astype(F32) * ret_branch
              + gates_ref[:, D_MODEL:].astype(F32) * mla_branch)
    o_ref[...] = x_ref[...] + _dot(merged.astype(BF16), wo_ref[...])


def _merge(x2, ret, mla, gates, w_ret_o, w_mla_o, w_out, tm):
    t = x2.shape[0]
    row = lambda w: pl.BlockSpec((tm, w), lambda i: (i, 0))
    sq = _resident((D_MODEL, D_MODEL))
    return pl.pallas_call(
        _merge_kernel,
        grid=(t // tm,),
        in_specs=[row(D_MODEL), row(RET_V_W), row(D_MODEL), row(2 * D_MODEL), sq, sq, sq],
        out_specs=row(D_MODEL),
        out_shape=jax.ShapeDtypeStruct((t, D_MODEL), F32),
        compiler_params=_params("arbitrary"),
        name="merge",
    )(x2, ret, mla, gates, w_ret_o, w_mla_o, w_out)


FFN_HALO = 8
FFN_NC = 256
FFN_U_BUFS = 4

def _ffn_kernel(x_ref, prev_ref, next_ref, g_ref, wup_ref, cw_ref, cb_ref, wdn_ref, o_ref, h_ref, u_ref,
                act_ref, *, ts, nblk):
    i = pl.program_id(1)
    g = g_ref[...]

    def rms(x):
        ms = jnp.mean(x * x, axis=-1, keepdims=True)
        return x * lax.rsqrt(ms + EPS) * g

    keep_prev = jnp.where(i > 0, 1.0, 0.0)
    keep_next = jnp.where(i < nblk - 1, 1.0, 0.0)
    xm = x_ref[0]
    h_ref[:FFN_HALO] = (rms(prev_ref[0]) * keep_prev).astype(BF16)
    h_ref[FFN_HALO:FFN_HALO + ts] = rms(xm).astype(BF16)
    h_ref[FFN_HALO + ts:] = (rms(next_ref[0]) * keep_next).astype(BF16)
    h = h_ref[...]

    def conv(slot, c0):
        u_ref[slot] = _dot(h, wup_ref[:, c0:c0 + FFN_NC])
        w = cw_ref[:, c0:c0 + FFN_NC]
        return (u_ref[slot, pl.ds(FFN_HALO - 1, ts), :] * w[0:1]
                + u_ref[slot, pl.ds(FFN_HALO, ts), :] * w[1:2]
                + u_ref[slot, pl.ds(FFN_HALO + 1, ts), :] * w[2:3]
                + cb_ref[:, c0:c0 + FFN_NC])

    for n in range(D_FF // FFN_NC):
        ca = n * FFN_NC
        ua = conv((2 * n) % FFN_U_BUFS, ca)
        ub = conv((2 * n + 1) % FFN_U_BUFS, D_FF + ca)
        act_ref[:, ca:ca + FFN_NC] = (ua * _sigmoid(ua) * ub).astype(BF16)
    o_ref[0] = xm + _dot(act_ref[...], wdn_ref[...])


def _ffn(x1, g_ffn, w_up, conv_w, conv_b, w_down, ts):
    bsz, seq, _ = x1.shape
    nblk = seq // ts
    hb = ts // FFN_HALO
    last_halo = seq // FFN_HALO - 1
    return pl.pallas_call(
        functools.partial(_ffn_kernel, ts=ts, nblk=nblk),
        grid=(bsz, nblk),
        in_specs=[pl.BlockSpec((1, ts, D_MODEL), lambda b, i: (b, i, 0)),
                  pl.BlockSpec((1, FFN_HALO, D_MODEL), lambda b, i: (b, jnp.maximum(i * hb - 1, 0), 0)),
                  pl.BlockSpec((1, FFN_HALO, D_MODEL), lambda b, i: (b, jnp.minimum((i + 1) * hb, last_halo), 0)),
                  _resident((1, D_MODEL)), _resident((D_MODEL, 2 * D_FF)),
                  _resident((3, 2 * D_FF)), _resident((1, 2 * D_FF)), _resident((D_FF, D_MODEL))],
        out_specs=pl.BlockSpec((1, ts, D_MODEL), lambda b, i: (b, i, 0)),
        out_shape=jax.ShapeDtypeStruct((bsz, seq, D_MODEL), F32),
        scratch_shapes=[pltpu.VMEM((ts + 2 * FFN_HALO, D_MODEL), BF16),
                        pltpu.VMEM((FFN_U_BUFS, ts + 2 * FFN_HALO, FFN_NC), F32),
                        pltpu.VMEM((ts, D_FF), BF16)],
        compiler_params=_params("arbitrary", "arbitrary"),
        name="ffn",
    )(x1, x1, x1, g_ffn, w_up, conv_w, conv_b, w_down)


def _rope_tables(seq):
    pos = jnp.arange(seq).astype(F32)[:, None]

    def cos_sin(d):
        inv = ROPE_BASE ** (-jnp.arange(0, d, 2, dtype=F32) / d)
        ang = pos * inv[None, :]
        return jnp.cos(ang), jnp.sin(ang)

    cr, sr = cos_sin(RET_DK)
    cm, sm = cos_sin(MLA_ROPE)
    zh = jnp.zeros_like(sm)
    zpad = jnp.zeros((seq, LANES - MLA_ROPE), F32)
    ret = (jnp.concatenate([cr, cr], 1), jnp.concatenate([-sr, sr], 1))
    mla = (jnp.concatenate([cm, cm, zpad], 1), jnp.concatenate([-sm, zh, zpad], 1),
           jnp.concatenate([zh, sm, zpad], 1), cm.T, sm.T)
    return ret, mla


def _tile(seq, want):
    t = min(seq, want)
    assert seq % t == 0, (seq, t)
    return t


def _trunk(x, p, depth):
    bsz, seq, _ = x.shape
    t = bsz * seq
    (cos_r, sin_r), mla_tabs = _rope_tables(seq)
    tm = _tile(seq, 512)
    for l in range(depth):
        w = {k: v[l] for k, v in p.items()}
        x2 = x.reshape(t, D_MODEL)
        qk, rv, rg, lat, gates = _inproj(x2, w["g_mix"], w["w_in"], cos_r, sin_r, seq, tm)
        ret = _retention(w["dec"], w["ret_gn_g"],
                         qk.reshape(bsz, seq, -1), rv.reshape(bsz, seq, -1), rg.reshape(bsz, seq, -1),
                         _tile(seq, 512))
        q, k, v = _mla_prep(lat.reshape(bsz, seq, -1), w, mla_tabs, _tile(seq, 512))
        mla = _attention(w["att_shift"], q, k, v, _tile(seq, 1024), _tile(seq, 512))
        x1 = _merge(x2, ret.reshape(t, -1), mla.reshape(t, -1), gates,
                    w["w_ret_o"], w["w_mla_o"], w["w_out"], tm)
        x = _ffn(x1.reshape(bsz, seq, D_MODEL), w["g_ffn"], w["w_up"], w["conv_w"], w["conv_b"],
                 w["w_down"], _tile(seq, 512))
    return x


def _prepare_weights(g_mix, w_in, ret_decay_fwd, ret_decay_bwd, ret_gn_g, w_ret_o, g_cq, w_uq, g_ckv,
                     w_ukv, g_qn, g_kn, w_mla_o, w_out, g_ffn, w_up, conv_w, conv_b, w_down):
    depth = w_in.shape[0]
    kr_end = 2 * RET_Q_W + 2 * RET_V_W + Q_LORA + KV_LORA + MLA_ROPE
    w_in_p = jnp.concatenate(
        [w_in[:, :, :kr_end], jnp.zeros((depth, D_MODEL, LANES - MLA_ROPE), w_in.dtype), w_in[:, :, kr_end:]],
        axis=2).astype(BF16)
    pad_h = MLA_QK_PAD - MLA_QK
    w_uq_p = jnp.pad(w_uq.reshape(depth, Q_LORA, MLA_HEADS, MLA_QK), ((0, 0), (0, 0), (0, 0), (0, pad_h)))
    w_uq_t = jnp.swapaxes(w_uq_p.reshape(depth, Q_LORA, MLA_HEADS * MLA_QK_PAD), 1, 2).astype(BF16)
    w_ukv_h = w_ukv.reshape(depth, KV_LORA, MLA_HEADS, MLA_NOPE + MLA_V)
    w_k = w_ukv_h[..., :MLA_NOPE].reshape(depth, KV_LORA, MLA_HEADS * MLA_NOPE).astype(BF16)
    w_v_t = jnp.swapaxes(w_ukv_h[..., MLA_NOPE:].reshape(depth, KV_LORA, MLA_HEADS * MLA_V), 1, 2).astype(BF16)
    qscale = LOG2_E * MLA_QK ** -0.5
    g_qn_rows = jnp.broadcast_to((jnp.pad(g_qn, ((0, 0), (0, pad_h))) * qscale)[..., None],
                                 (depth, MLA_QK_PAD, LANES)).astype(F32)
    att_shift = (MLA_QK * qscale * jnp.max(jnp.abs(g_qn), axis=1) * jnp.max(jnp.abs(g_kn), axis=1))
    att_shift = att_shift.astype(F32).reshape(depth, 1, 1)
    dec = jnp.stack([ret_decay_fwd, ret_decay_bwd], axis=1).astype(F32)
    dec = jnp.broadcast_to(dec[..., None], (depth, 2, RET_HEADS, LANES))
    row = lambda a: a[:, None, :].astype(F32)
    return {
        "g_mix": row(g_mix), "w_in": w_in_p, "dec": dec, "ret_gn_g": row(ret_gn_g),
        "w_ret_o": w_ret_o.astype(BF16), "g_cq": row(g_cq), "w_uq_t": w_uq_t, "g_ckv": row(g_ckv),
        "w_k": w_k, "w_v_t": w_v_t, "att_shift": att_shift,
        "g_qn_rows": g_qn_rows, "g_kn": row(jnp.pad(g_kn, ((0, 0), (0, pad_h)))),
        "w_mla_o": w_mla_o.astype(BF16), "w_out": w_out.astype(BF16), "g_ffn": row(g_ffn),
        "w_up": w_up.astype(BF16), "conv_w": conv_w.astype(F32), "conv_b": row(conv_b),
        "w_down": w_down.astype(BF16),
    }


def kernel(x_prompt, x_sample, g_mix, w_in, ret_decay_fwd, ret_decay_bwd, ret_gn_g, w_ret_o, g_cq, w_uq,
           g_ckv, w_ukv, g_qn, g_kn, w_mla_o, w_out, g_ffn, w_up, conv_w, conv_b, w_down):
    depth = w_in.shape[0]
    p = _prepare_weights(g_mix, w_in, ret_decay_fwd, ret_decay_bwd, ret_gn_g, w_ret_o, g_cq, w_uq, g_ckv,
                         w_ukv, g_qn, g_kn, w_mla_o, w_out, g_ffn, w_up, conv_w, conv_b, w_down)
    return (_trunk(x_prompt, p, depth), _trunk(x_sample, p, depth))
```

```python
import functools

import jax
import jax.numpy as jnp
from jax import lax
from jax.experimental import pallas as pl
from jax.experimental.pallas import tpu as pltpu

D_MODEL = 1024
RET_HEADS = 4
RET_DK = 128
RET_DV = 256
RET_CHUNK = 128
MLA_HEADS = 8
MLA_NOPE = 128
MLA_ROPE = 64
MLA_V = 128
MLA_QK = MLA_NOPE + MLA_ROPE
MLA_QK_PAD = 256
Q_LORA = 384
KV_LORA = 256
D_FF = 2816
ROPE_BASE = 10000.0
EPS = 1e-6
LOG2_E = 1.4426950408889634

RET_Q_W = RET_HEADS * RET_DK
RET_V_W = RET_HEADS * RET_DV
LATENT_W = Q_LORA + KV_LORA + 128
IN_W_PAD = 2 * RET_Q_W + 2 * RET_V_W + LATENT_W + 2 * D_MODEL

LANES = 128
VMEM_LIMIT_BYTES = 56 * 1024 * 1024

F32 = jnp.float32
BF16 = jnp.bfloat16


def _params(*semantics):
    return pltpu.CompilerParams(dimension_semantics=semantics, vmem_limit_bytes=VMEM_LIMIT_BYTES)


def _resident(shape):
    return pl.BlockSpec(shape, lambda *_: (0,) * len(shape), pipeline_mode=pl.Buffered(1))


def _sigmoid(x):
    return 1.0 / (1.0 + jnp.exp(-x))


def _dot(a, b):
    return jnp.dot(a, b, preferred_element_type=F32)


def _dot_nt(a, b):
    return lax.dot_general(a, b, (((1,), (1,)), ((), ())), preferred_element_type=F32)


def _dot_tn(a, b):
    return lax.dot_general(a, b, (((0,), (0,)), ((), ())), preferred_element_type=F32)


def _inproj_kernel(x_ref, g_ref, w_ref, cos_ref, sin_ref,
                   qk_ref, v_ref, rg_ref, lat_ref, gates_ref):
    xf = x_ref[...]
    ms = jnp.mean(xf * xf, axis=-1, keepdims=True)
    h = (xf * lax.rsqrt(ms + EPS) * g_ref[...]).astype(BF16)
    cos = cos_ref[...]
    sin = sin_ref[...]

    def proj(c0, width):
        return _dot(h, w_ref[:, c0:c0 + width])

    for seg, scale in ((0, None), (1, RET_DK ** -0.5)):
        acc = proj(seg * RET_Q_W, RET_Q_W)
        for hd in range(RET_HEADS):
            xs = acc[:, hd * RET_DK:(hd + 1) * RET_DK]
            r = xs * cos + pltpu.roll(xs, RET_DK // 2, 1) * sin
            if scale is not None:
                r = r * scale
            c0 = seg * RET_Q_W + hd * RET_DK
            qk_ref[:, c0:c0 + RET_DK] = r.astype(BF16)
    base = 2 * RET_Q_W
    half = RET_V_W // 2
    for j in range(2):
        v_ref[:, j * half:(j + 1) * half] = proj(base + j * half, half).astype(BF16)
    base += RET_V_W
    for j in range(2):
        a = proj(base + j * half, half)
        rg_ref[:, j * half:(j + 1) * half] = (a * _sigmoid(a)).astype(BF16)
    base += RET_V_W
    lat_ref[...] = proj(base, LATENT_W).astype(BF16)
    base += LATENT_W
    for j in range(4):
        a = proj(base + j * half, half)
        gates_ref[:, j * half:(j + 1) * half] = _sigmoid(a).astype(BF16)


def _inproj(x2, g_mix, w_in_p, cos_r, sin_r, seq, tm):
    t = x2.shape[0]
    nseq = seq // tm
    row = lambda w: pl.BlockSpec((tm, w), lambda i: (i, 0))
    pos = pl.BlockSpec((tm, LANES), lambda i: (i % nseq, 0))
    out_w = (2 * RET_Q_W, RET_V_W, RET_V_W, LATENT_W, 2 * D_MODEL)
    return pl.pallas_call(
        _inproj_kernel,
        grid=(t // tm,),
        in_specs=[row(D_MODEL), _resident((1, D_MODEL)), _resident((D_MODEL, IN_W_PAD)), pos, pos],
        out_specs=[row(w) for w in out_w],
        out_shape=[jax.ShapeDtypeStruct((t, w), BF16) for w in out_w],
        compiler_params=_params("arbitrary"),
        name="inproj",
    )(x2, g_mix, w_in_p, cos_r, sin_r)


_T_DMAT, _T_QF, _T_QB, _T_KF, _T_KB, _T_CF, _T_CB = range(7)


def _retention_kernel(dec_ref, gn_ref, qk_ref, v_ref, rg_ref, o_ref,
                      tab_ref, sf_ref, sb_ref, sball_ref, *, ts, nblk):
    c = RET_CHUNK
    ncb = ts // c
    b = pl.program_id(0)
    ph = pl.program_id(1)
    i = pl.program_id(2)

    @pl.when((b == 0) & (ph == 0) & (i == 0))
    def _tables():
        row = lax.broadcasted_iota(jnp.int32, (c, c), 0).astype(F32)
        col = lax.broadcasted_iota(jnp.int32, (c, c), 1).astype(F32)
        diff = row - col
        for hd in range(RET_HEADS):
            def log_sigmoid(d):
                return jnp.minimum(d, 0.0) - jnp.log1p(jnp.exp(-jnp.abs(d)))
            lgf = log_sigmoid(dec_ref[0, hd:hd + 1, :])
            lgb = log_sigmoid(dec_ref[1, hd:hd + 1, :])
            tab_ref[hd, _T_DMAT] = jnp.where(diff >= 0.0,
                                             jnp.exp(lgf * jnp.maximum(diff, 0.0)),
                                             jnp.exp(lgb * jnp.maximum(-diff, 0.0)))
            tab_ref[hd, _T_QF] = jnp.exp(lgf * (row + 1.0))
            tab_ref[hd, _T_QB] = jnp.exp(lgb * (c - row))
            tab_ref[hd, _T_KF] = jnp.exp(lgf * (c - 1.0 - row))
            tab_ref[hd, _T_KB] = jnp.exp(lgb * row)
            tab_ref[hd, _T_CF] = jnp.exp(lgf * (row * 0.0 + c))
            tab_ref[hd, _T_CB] = jnp.exp(lgb * (row * 0.0 + c))

    def chunk_decay(hd, which):
        t = tab_ref[hd, which]
        return jnp.concatenate([t, t], axis=1)

    @pl.when(ph == 0)
    def _backward_states():
        @pl.when(i == 0)
        def _():
            sb_ref[...] = jnp.zeros_like(sb_ref)
        blk = nblk - 1 - i
        for cb in reversed(range(ncb)):
            r0 = cb * c
            for hd in range(RET_HEADS):
                k = qk_ref[0, r0:r0 + c, RET_Q_W + hd * RET_DK:RET_Q_W + (hd + 1) * RET_DK]
                v = v_ref[0, r0:r0 + c, hd * RET_DV:(hd + 1) * RET_DV]
                st = sb_ref[hd]
                sball_ref[blk * ncb + cb, hd] = st.astype(BF16)
                kd = (k.astype(F32) * tab_ref[hd, _T_KB]).astype(BF16)
                sb_ref[hd] = st * chunk_decay(hd, _T_CB) + _dot_tn(kd, v)

    @pl.when(ph == 1)
    def _forward():
        @pl.when(i == 0)
        def _():
            sf_ref[...] = jnp.zeros_like(sf_ref)
        for cb in range(ncb):
            r0 = cb * c
            for hd in range(RET_HEADS):
                q = qk_ref[0, r0:r0 + c, hd * RET_DK:(hd + 1) * RET_DK]
                k = qk_ref[0, r0:r0 + c, RET_Q_W + hd * RET_DK:RET_Q_W + (hd + 1) * RET_DK]
                v = v_ref[0, r0:r0 + c, hd * RET_DV:(hd + 1) * RET_DV]
                qf32 = q.astype(F32)
                kf32 = k.astype(F32)
                sd = (_dot_nt(q, k) * tab_ref[hd, _T_DMAT]).astype(BF16)
                st = sf_ref[hd]
                out = _dot(sd, v)
                out = out + _dot((qf32 * tab_ref[hd, _T_QF]).astype(BF16), st.astype(BF16))
                out = out + _dot((qf32 * tab_ref[hd, _T_QB]).astype(BF16), sball_ref[i * ncb + cb, hd])
                kd = (kf32 * tab_ref[hd, _T_KF]).astype(BF16)
                sf_ref[hd] = st * chunk_decay(hd, _T_CF) + _dot_tn(kd, v)
                mu = jnp.mean(out, axis=-1, keepdims=True)
                xc = out - mu
                var = jnp.mean(xc * xc, axis=-1, keepdims=True)
                y = xc * lax.rsqrt(var + EPS) * gn_ref[:, hd * RET_DV:(hd + 1) * RET_DV]
                gate = rg_ref[0, r0:r0 + c, hd * RET_DV:(hd + 1) * RET_DV].astype(F32)
                o_ref[0, r0:r0 + c, hd * RET_DV:(hd + 1) * RET_DV] = (y * gate).astype(BF16)


def _retention(dec, gn_g, qk, v, rg, ts):
    bsz, seq, _ = qk.shape
    nblk = seq // ts
    nchunk = seq // RET_CHUNK
    sweep = lambda b, ph, i: (b, i * ph + (nblk - 1 - i) * (1 - ph), 0)
    fwd_only = lambda b, ph, i: (b, i * ph, 0)
    return pl.pallas_call(
        functools.partial(_retention_kernel, ts=ts, nblk=nblk),
        grid=(bsz, 2, nblk),
        in_specs=[_resident((2, RET_HEADS, LANES)), _resident((1, RET_V_W)),
                  pl.BlockSpec((1, ts, 2 * RET_Q_W), sweep),
                  pl.BlockSpec((1, ts, RET_V_W), sweep),
                  pl.BlockSpec((1, ts, RET_V_W), fwd_only)],
        out_specs=pl.BlockSpec((1, ts, RET_V_W), fwd_only),
        out_shape=jax.ShapeDtypeStruct((bsz, seq, RET_V_W), BF16),
        scratch_shapes=[pltpu.VMEM((RET_HEADS, 7, RET_CHUNK, RET_CHUNK), F32),
                        pltpu.VMEM((RET_HEADS, RET_DK, RET_DV), F32),
                        pltpu.VMEM((RET_HEADS, RET_DK, RET_DV), F32),
                        pltpu.VMEM((nchunk, RET_HEADS, RET_DK, RET_DV), BF16)],
        compiler_params=_params("arbitrary", "arbitrary", "arbitrary"),
        name="retention",
    )(dec, gn_g, qk, v, rg)


def _rope64(x, cos, sin_lo, sin_hi):
    return x * cos + pltpu.roll(x, LANES - MLA_ROPE // 2, 1) * sin_lo + pltpu.roll(x, MLA_ROPE // 2, 1) * sin_hi


def _mla_prep_kernel(lat_ref, gcq_ref, wuqt_ref, gckv_ref, wk_ref, wvt_ref, gqn_ref, gkn_ref,
                     cos_ref, slo_ref, shi_ref, cost_ref, sint_ref, qt_ref, k_ref, vt_ref):
    ts = lat_ref.shape[1]

    def rms(x, g):
        ms = jnp.mean(x * x, axis=-1, keepdims=True)
        return x * lax.rsqrt(ms + EPS) * g

    cqn = rms(lat_ref[0, :, :Q_LORA].astype(F32), gcq_ref[...])
    ckvn = rms(lat_ref[0, :, Q_LORA:Q_LORA + KV_LORA].astype(F32), gckv_ref[...])
    kr = lat_ref[0, :, Q_LORA + KV_LORA:].astype(F32)
    cqn_t = cqn.T.astype(BF16)
    ckvn_t = ckvn.T.astype(BF16)

    vt_all = _dot(wvt_ref[...], ckvn_t)
    for hd in range(MLA_HEADS):
        for c in range(ts // LANES):
            vt_ref[0, hd, c] = vt_all[hd * MLA_V:(hd + 1) * MLA_V, c * LANES:(c + 1) * LANES].astype(BF16)

    half = MLA_ROPE // 2
    r0 = MLA_NOPE
    for hd in range(MLA_HEADS):
        qh = _dot(wuqt_ref[hd * MLA_QK_PAD:(hd + 1) * MLA_QK_PAD, :], cqn_t)
        for c in range(ts // LANES):
            lanes = slice(c * LANES, (c + 1) * LANES)
            qc = qh[:, lanes]
            ssq = jnp.sum(qc * qc, axis=0, keepdims=True) * (1.0 / MLA_QK)
            qn = qc * lax.rsqrt(ssq + EPS) * gqn_ref[...]
            x1 = qn[r0:r0 + half]
            x2 = qn[r0 + half:r0 + 2 * half]
            cos = cost_ref[:, lanes]
            sin = sint_ref[:, lanes]
            qt_ref[0, hd, :r0, lanes] = qn[:r0].astype(BF16)
            qt_ref[0, hd, r0:r0 + half, lanes] = (x1 * cos - x2 * sin).astype(BF16)
            qt_ref[0, hd, r0 + half:r0 + 2 * half, lanes] = (x2 * cos + x1 * sin).astype(BF16)
            qt_ref[0, hd, r0 + 2 * half:, lanes] = jnp.zeros((MLA_QK_PAD - MLA_QK, LANES), BF16)

    cos = cos_ref[...]
    slo = slo_ref[...]
    shi = shi_ref[...]
    gkn = gkn_ref[...]
    kr = _rope64(kr, cos, slo, shi)
    kr_sq = jnp.sum(kr * kr, axis=-1, keepdims=True)
    kr_rot = _rope64(kr * gkn[:, MLA_NOPE:], cos, slo, shi)
    kn_all = _dot(ckvn.astype(BF16), wk_ref[...])
    for hd in range(MLA_HEADS):
        kn = kn_all[:, hd * MLA_NOPE:(hd + 1) * MLA_NOPE]
        ssk = (jnp.sum(kn * kn, axis=-1, keepdims=True) + kr_sq) * (1.0 / MLA_QK)
        rk = lax.rsqrt(ssk + EPS)
        k_ref[0, hd, :, :MLA_NOPE] = (kn * rk * gkn[:, :MLA_NOPE]).astype(BF16)
        k_ref[0, hd, :, MLA_NOPE:] = (kr_rot * rk).astype(BF16)


def _mla_prep(lat, w, tabs, ts):
    bsz, seq, _ = lat.shape
    cos_m, slo_m, shi_m, cos_t, sin_t = tabs
    pos = pl.BlockSpec((ts, LANES), lambda b, i: (i, 0))
    pos_t = pl.BlockSpec((MLA_ROPE // 2, ts), lambda b, i: (0, i))
    return pl.pallas_call(
        _mla_prep_kernel,
        grid=(bsz, seq // ts),
        in_specs=[pl.BlockSpec((1, ts, LATENT_W), lambda b, i: (b, i, 0)),
                  _resident((1, Q_LORA)), _resident((MLA_HEADS * MLA_QK_PAD, Q_LORA)),
                  _resident((1, KV_LORA)), _resident((KV_LORA, MLA_HEADS * MLA_NOPE)),
                  _resident((MLA_HEADS * MLA_V, KV_LORA)),
                  _resident((MLA_QK_PAD, LANES)), _resident((1, MLA_QK_PAD)),
                  pos, pos, pos, pos_t, pos_t],
        out_specs=[pl.BlockSpec((1, MLA_HEADS, MLA_QK_PAD, ts), lambda b, i: (b, 0, 0, i)),
                   pl.BlockSpec((1, MLA_HEADS, ts, MLA_QK_PAD), lambda b, i: (b, 0, i, 0)),
                   pl.BlockSpec((1, MLA_HEADS, ts // LANES, MLA_V, LANES), lambda b, i: (b, 0, i, 0, 0))],
        out_shape=[jax.ShapeDtypeStruct((bsz, MLA_HEADS, MLA_QK_PAD, seq), BF16),
                   jax.ShapeDtypeStruct((bsz, MLA_HEADS, seq, MLA_QK_PAD), BF16),
                   jax.ShapeDtypeStruct((bsz, MLA_HEADS, seq // LANES, MLA_V, LANES), BF16)],
        compiler_params=_params("arbitrary", "arbitrary"),
        name="mla_prep",
    )(lat, w["g_cq"], w["w_uq_t"], w["g_ckv"], w["w_k"], w["w_v_t"], w["g_qn_rows"], w["g_kn"],
      cos_m, slo_m, shi_m, cos_t, sin_t)


ATT_UNROLL = 16
ATT_UNROLL_RM = 4
ATT_PITCH_PAD = LANES
ATT_FIXED_SHIFT_MAX = 60.0


def _attention_kernel(shift_ref, qt_ref, k_ref, vt_ref, o_ref, s0_ref, s1_ref, *, tk, nkv):
    qt = qt_ref[0, 0]
    tq = qt.shape[1]
    sub = tk // LANES
    s_refs = (s0_ref, s1_ref)

    def scores(j):
        k0 = j * tk
        if not isinstance(j, int):
            k0 = pl.multiple_of(k0, tk)
        return _dot(k_ref[0, 0, pl.ds(k0, tk), :], qt)

    def values_t(j):
        vt3 = vt_ref[0, 0, pl.ds(j * sub, sub)]
        return jnp.concatenate([vt3[i] for i in range(sub)], axis=1)

    def blocks(n_trips, unroll, trip_fn, tail_fn, carry):
        carry = lax.fori_loop(0, n_trips, trip_fn, carry)
        for j in range(n_trips * unroll, nkv):
            carry = tail_fn(j, carry)
        return carry

    shift = shift_ref[0, 0]
    fixed_ok = shift <= ATT_FIXED_SHIFT_MAX

    @pl.when(fixed_ok)
    def _fixed_shift():
        def block(j, carry):
            l, acc = carry
            e = jnp.exp2(scores(j) - shift)
            return l + jnp.sum(e, axis=0, keepdims=True), acc + _dot(values_t(j), e.astype(BF16))

        def trip(i, carry):
            for u in range(ATT_UNROLL):
                carry = block(i * ATT_UNROLL + u, carry)
            return carry

        carry = (jnp.zeros((1, tq), F32), jnp.zeros((MLA_V, tq), F32))
        l, acc = blocks(nkv // ATT_UNROLL, ATT_UNROLL, trip, block, carry)
        o_ref[0] = (acc / l).T.astype(BF16)

    @pl.when(jnp.logical_not(fixed_ok))
    def _running_max():
        def update(s_ref, j, carry):
            m, l, acc = carry
            s = s_ref[:, :tq]
            m_new = jnp.maximum(m, jnp.max(s, axis=0, keepdims=True))
            p = jnp.exp2(s - m_new)
            alpha = jnp.exp2(m - m_new)
            l = alpha * l + jnp.sum(p, axis=0, keepdims=True)
            acc = alpha * acc + _dot(values_t(j), p.astype(BF16))
            return m_new, l, acc

        def trip(i, carry):
            j = i * ATT_UNROLL_RM
            for u in range(ATT_UNROLL_RM):
                s_refs[(u + 1) % 2][:, :tq] = scores(j + u + 1)
                carry = update(s_refs[u % 2], j + u, carry)
            return carry

        def tail(j, carry):
            if j + 1 < nkv:
                s_refs[(j + 1) % 2][:, :tq] = scores(j + 1)
            return update(s_refs[j % 2], j, carry)

        carry = (jnp.full((1, tq), -jnp.inf, F32), jnp.zeros((1, tq), F32), jnp.zeros((MLA_V, tq), F32))
        s0_ref[:, :tq] = scores(0)
        _, l, acc = blocks((nkv - 1) // ATT_UNROLL_RM, ATT_UNROLL_RM, trip, tail, carry)
        o_ref[0] = (acc / l).T.astype(BF16)


def _attention(shift, qt, k, vt, tq, tk):
    bsz, nh, seq, _ = k.shape
    nkv = seq // tk
    assert nkv >= 2 and nkv % 2 == 0 and tk % LANES == 0, (seq, tk)
    return pl.pallas_call(
        functools.partial(_attention_kernel, tk=tk, nkv=nkv),
        grid=(bsz, nh, seq // tq),
        in_specs=[pl.BlockSpec(memory_space=pltpu.SMEM),
                  pl.BlockSpec((1, 1, MLA_QK_PAD, tq), lambda b, h, i: (b, h, 0, i)),
                  pl.BlockSpec((1, 1, seq, MLA_QK_PAD), lambda b, h, i: (b, h, 0, 0)),
                  pl.BlockSpec((1, 1, seq // LANES, MLA_V, LANES), lambda b, h, i: (b, h, 0, 0, 0))],
        out_specs=pl.BlockSpec((1, tq, MLA_V), lambda b, h, i: (b, i, h)),
        out_shape=jax.ShapeDtypeStruct((bsz, seq, nh * MLA_V), BF16),
        scratch_shapes=[pltpu.VMEM((tk, tq + ATT_PITCH_PAD), F32), pltpu.VMEM((tk, tq + ATT_PITCH_PAD), F32)],
        compiler_params=_params("arbitrary", "arbitrary", "arbitrary"),
        name="attention",
    )(shift, qt, k, vt)


def _merge_kernel(x_ref, ret_ref, mla_ref, gates_ref, wr_ref, wm_ref, wo_ref, o_ref):
    ret_branch = _dot(ret_ref[...], wr_ref[...])
    mla_branch = _dot(mla_ref[...], wm_ref[...])
    merged = (gates_ref[:, :D_MODEL].astype(F32) * ret_branch
              + gates_ref[:, D_MODEL:].astype(F32) * mla_branch)
    o_ref[...] = x_ref[...] + _dot(merged.astype(BF16), wo_ref[...])


def _merge(x2, ret, mla, gates, w_ret_o, w_mla_o, w_out, tm):
    t = x2.shape[0]
    row = lambda w: pl.BlockSpec((tm, w), lambda i: (i, 0))
    sq = _resident((D_MODEL, D_MODEL))
    return pl.pallas_call(
        _merge_kernel,
        grid=(t // tm,),
        in_specs=[row(D_MODEL), row(RET_V_W), row(D_MODEL), row(2 * D_MODEL), sq, sq, sq],
        out_specs=row(D_MODEL),
        out_shape=jax.ShapeDtypeStruct((t, D_MODEL), F32),
        compiler_params=_params("arbitrary"),
        name="merge",
    )(x2, ret, mla, gates, w_ret_o, w_mla_o, w_out)


FFN_HALO = 8
FFN_NC = 256
FFN_U_BUFS = 4

def _ffn_kernel(x_ref, prev_ref, next_ref, g_ref, wup_ref, cw_ref, cb_ref, wdn_ref, o_ref, h_ref, u_ref,
                act_ref, *, ts, nblk):
    i = pl.program_id(1)
    g = g_ref[...]

    def rms(x):
        ms = jnp.mean(x * x, axis=-1, keepdims=True)
        return x * lax.rsqrt(ms + EPS) * g

    keep_prev = jnp.where(i > 0, 1.0, 0.0)
    keep_next = jnp.where(i < nblk - 1, 1.0, 0.0)
    xm = x_ref[0]
    h_ref[:FFN_HALO] = (rms(prev_ref[0]) * keep_prev).astype(BF16)
    h_ref[FFN_HALO:FFN_HALO + ts] = rms(xm).astype(BF16)
    h_ref[FFN_HALO + ts:] = (rms(next_ref[0]) * keep_next).astype(BF16)
    h = h_ref[...]

    def conv(slot, c0, l0):
        w = cw_ref[:, c0 + l0:c0 + l0 + LANES]
        lanes = pl.ds(l0, LANES)
        return (u_ref[slot, pl.ds(FFN_HALO - 1, ts), lanes] * w[0:1]
                + u_ref[slot, pl.ds(FFN_HALO, ts), lanes] * w[1:2]
                + u_ref[slot, pl.ds(FFN_HALO + 1, ts), lanes] * w[2:3]
                + cb_ref[:, c0 + l0:c0 + l0 + LANES])

    for n in range(D_FF // FFN_NC):
        ca = n * FFN_NC
        sa = (2 * n) % FFN_U_BUFS
        sb = (2 * n + 1) % FFN_U_BUFS
        u_ref[sa] = _dot(h, wup_ref[:, ca:ca + FFN_NC])
        u_ref[sb] = _dot(h, wup_ref[:, D_FF + ca:D_FF + ca + FFN_NC])
        for l0 in range(0, FFN_NC, LANES):
            ua = conv(sa, ca, l0)
            ub = conv(sb, D_FF + ca, l0)
            act_ref[:, ca + l0:ca + l0 + LANES] = (ua * _sigmoid(ua) * ub).astype(BF16)
    o_ref[0] = xm + _dot(act_ref[...], wdn_ref[...])


def _ffn(x1, g_ffn, w_up, conv_w, conv_b, w_down, ts):
    bsz, seq, _ = x1.shape
    nblk = seq // ts
    hb = ts // FFN_HALO
    last_halo = seq // FFN_HALO - 1
    return pl.pallas_call(
        functools.partial(_ffn_kernel, ts=ts, nblk=nblk),
        grid=(bsz, nblk),
        in_specs=[pl.BlockSpec((1, ts, D_MODEL), lambda b, i: (b, i, 0)),
                  pl.BlockSpec((1, FFN_HALO, D_MODEL), lambda b, i: (b, jnp.maximum(i * hb - 1, 0), 0)),
                  pl.BlockSpec((1, FFN_HALO, D_MODEL), lambda b, i: (b, jnp.minimum((i + 1) * hb, last_halo), 0)),
                  _resident((1, D_MODEL)), _resident((D_MODEL, 2 * D_FF)),
                  _resident((3, 2 * D_FF)), _resident((1, 2 * D_FF)), _resident((D_FF, D_MODEL))],
        out_specs=pl.BlockSpec((1, ts, D_MODEL), lambda b, i: (b, i, 0)),
        out_shape=jax.ShapeDtypeStruct((bsz, seq, D_MODEL), F32),
        scratch_shapes=[pltpu.VMEM((ts + 2 * FFN_HALO, D_MODEL), BF16),
                        pltpu.VMEM((FFN_U_BUFS, ts + 2 * FFN_HALO, FFN_NC), F32),
                        pltpu.VMEM((ts, D_FF), BF16)],
        compiler_params=_params("arbitrary", "arbitrary"),
        name="ffn",
    )(x1, x1, x1, g_ffn, w_up, conv_w, conv_b, w_down)


def _rope_tables(seq):
    pos = jnp.arange(seq).astype(F32)[:, None]

    def cos_sin(d):
        inv = ROPE_BASE ** (-jnp.arange(0, d, 2, dtype=F32) / d)
        ang = pos * inv[None, :]
        return jnp.cos(ang), jnp.sin(ang)

    cr, sr = cos_sin(RET_DK)
    cm, sm = cos_sin(MLA_ROPE)
    zh = jnp.zeros_like(sm)
    zpad = jnp.zeros((seq, LANES - MLA_ROPE), F32)
    ret = (jnp.concatenate([cr, cr], 1), jnp.concatenate([-sr, sr], 1))
    mla = (jnp.concatenate([cm, cm, zpad], 1), jnp.concatenate([-sm, zh, zpad], 1),
           jnp.concatenate([zh, sm, zpad], 1), cm.T, sm.T)
    return ret, mla


def _tile(seq, want):
    t = min(seq, want)
    assert seq % t == 0, (seq, t)
    return t


def _trunk(x, p, depth):
    bsz, seq, _ = x.shape
    t = bsz * seq
    (cos_r, sin_r), mla_tabs = _rope_tables(seq)
    tm = _tile(seq, 512)
    for l in range(depth):
        w = {k: v[l] for k, v in p.items()}
        x2 = x.reshape(t, D_MODEL)
        qk, rv, rg, lat, gates = _inproj(x2, w["g_mix"], w["w_in"], cos_r, sin_r, seq, tm)
        ret = _retention(w["dec"], w["ret_gn_g"],
                         qk.reshape(bsz, seq, -1), rv.reshape(bsz, seq, -1), rg.reshape(bsz, seq, -1),
                         _tile(seq, 512))
        q, k, v = _mla_prep(lat.reshape(bsz, seq, -1), w, mla_tabs, _tile(seq, 512))
        mla = _attention(w["att_shift"], q, k, v, _tile(seq, 1024), _tile(seq, 512))
        x1 = _merge(x2, ret.reshape(t, -1), mla.reshape(t, -1), gates,
                    w["w_ret_o"], w["w_mla_o"], w["w_out"], tm)
        x = _ffn(x1.reshape(bsz, seq, D_MODEL), w["g_ffn"], w["w_up"], w["conv_w"], w["conv_b"],
                 w["w_down"], _tile(seq, 1024))
    return x


def _prepare_weights(g_mix, w_in, ret_decay_fwd, ret_decay_bwd, ret_gn_g, w_ret_o, g_cq, w_uq, g_ckv,
                     w_ukv, g_qn, g_kn, w_mla_o, w_out, g_ffn, w_up, conv_w, conv_b, w_down):
    depth = w_in.shape[0]
    kr_end = 2 * RET_Q_W + 2 * RET_V_W + Q_LORA + KV_LORA + MLA_ROPE
    w_in_p = jnp.concatenate(
        [w_in[:, :, :kr_end], jnp.zeros((depth, D_MODEL, LANES - MLA_ROPE), w_in.dtype), w_in[:, :, kr_end:]],
        axis=2).astype(BF16)
    pad_h = MLA_QK_PAD - MLA_QK
    w_uq_p = jnp.pad(w_uq.reshape(depth, Q_LORA, MLA_HEADS, MLA_QK), ((0, 0), (0, 0), (0, 0), (0, pad_h)))
    w_uq_t = jnp.swapaxes(w_uq_p.reshape(depth, Q_LORA, MLA_HEADS * MLA_QK_PAD), 1, 2).astype(BF16)
    w_ukv_h = w_ukv.reshape(depth, KV_LORA, MLA_HEADS, MLA_NOPE + MLA_V)
    w_k = w_ukv_h[..., :MLA_NOPE].reshape(depth, KV_LORA, MLA_HEADS * MLA_NOPE).astype(BF16)
    w_v_t = jnp.swapaxes(w_ukv_h[..., MLA_NOPE:].reshape(depth, KV_LORA, MLA_HEADS * MLA_V), 1, 2).astype(BF16)
    qscale = LOG2_E * MLA_QK ** -0.5
    g_qn_rows = jnp.broadcast_to((jnp.pad(g_qn, ((0, 0), (0, pad_h))) * qscale)[..., None],
                                 (depth, MLA_QK_PAD, LANES)).astype(F32)
    att_shift = (MLA_QK * qscale * jnp.max(jnp.abs(g_qn), axis=1) * jnp.max(jnp.abs(g_kn), axis=1))
    att_shift = att_shift.astype(F32).reshape(depth, 1, 1)
    dec = jnp.stack([ret_decay_fwd, ret_decay_bwd], axis=1).astype(F32)
    dec = jnp.broadcast_to(dec[..., None], (depth, 2, RET_HEADS, LANES))
    row = lambda a: a[:, None, :].astype(F32)
    return {
        "g_mix": row(g_mix), "w_in": w_in_p, "dec": dec, "ret_gn_g": row(ret_gn_g),
        "w_ret_o": w_ret_o.astype(BF16), "g_cq": row(g_cq), "w_uq_t": w_uq_t, "g_ckv": row(g_ckv),
        "w_k": w_k, "w_v_t": w_v_t, "att_shift": att_shift,
        "g_qn_rows": g_qn_rows, "g_kn": row(jnp.pad(g_kn, ((0, 0), (0, pad_h)))),
        "w_mla_o": w_mla_o.astype(BF16), "w_out": w_out.astype(BF16), "g_ffn": row(g_ffn),
        "w_up": w_up.astype(BF16), "conv_w": conv_w.astype(F32), "conv_b": row(conv_b),
        "w_down": w_down.astype(BF16),
    }


def kernel(x_prompt, x_sample, g_mix, w_in, ret_decay_fwd, ret_decay_bwd, ret_gn_g, w_ret_o, g_cq, w_uq,
           g_ckv, w_ukv, g_qn, g_kn, w_mla_o, w_out, g_ffn, w_up, conv_w, conv_b, w_down):
    depth = w_in.shape[0]
    p = _prepare_weights(g_mix, w_in, ret_decay_fwd, ret_decay_bwd, ret_gn_g, w_ret_o, g_cq, w_uq, g_ckv,
                         w_ukv, g_qn, g_kn, w_mla_o, w_out, g_ffn, w_up, conv_w, conv_b, w_down)
    return (_trunk(x_prompt, p, depth), _trunk(x_sample, p, depth))
```

```python
import functools

import jax
import jax.numpy as jnp
from jax import lax
from jax.experimental import pallas as pl
from jax.experimental.pallas import tpu as pltpu

D_MODEL = 1024
RET_HEADS = 4
RET_DK = 128
RET_DV = 256
RET_CHUNK = 128
MLA_HEADS = 8
MLA_NOPE = 128
MLA_ROPE = 64
MLA_V = 128
MLA_QK = MLA_NOPE + MLA_ROPE
MLA_QK_PAD = 256
Q_LORA = 384
KV_LORA = 256
D_FF = 2816
ROPE_BASE = 10000.0
EPS = 1e-6
LOG2_E = 1.4426950408889634

RET_Q_W = RET_HEADS * RET_DK
RET_V_W = RET_HEADS * RET_DV
LATENT_W = Q_LORA + KV_LORA + 128
IN_W_PAD = 2 * RET_Q_W + 2 * RET_V_W + LATENT_W + 2 * D_MODEL

LANES = 128
VMEM_LIMIT_BYTES = 56 * 1024 * 1024

F32 = jnp.float32
BF16 = jnp.bfloat16


def _params(*semantics):
    return pltpu.CompilerParams(dimension_semantics=semantics, vmem_limit_bytes=VMEM_LIMIT_BYTES)


def _resident(shape):
    return pl.BlockSpec(shape, lambda *_: (0,) * len(shape), pipeline_mode=pl.Buffered(1))


def _sigmoid(x):
    return 1.0 / (1.0 + jnp.exp(-x))


def _dot(a, b):
    return jnp.dot(a, b, preferred_element_type=F32)


def _dot_nt(a, b):
    return lax.dot_general(a, b, (((1,), (1,)), ((), ())), preferred_element_type=F32)


def _dot_tn(a, b):
    return lax.dot_general(a, b, (((0,), (0,)), ((), ())), preferred_element_type=F32)


def _inproj_kernel(x_ref, g_ref, w_ref, cos_ref, sin_ref,
                   qk_ref, v_ref, rg_ref, lat_ref, gates_ref):
    xf = x_ref[...]
    ms = jnp.mean(xf * xf, axis=-1, keepdims=True)
    h = (xf * lax.rsqrt(ms + EPS) * g_ref[...]).astype(BF16)
    cos = cos_ref[...]
    sin = sin_ref[...]

    def proj(c0, width):
        return _dot(h, w_ref[:, c0:c0 + width])

    for seg, scale in ((0, None), (1, RET_DK ** -0.5)):
        acc = proj(seg * RET_Q_W, RET_Q_W)
        for hd in range(RET_HEADS):
            xs = acc[:, hd * RET_DK:(hd + 1) * RET_DK]
            r = xs * cos + pltpu.roll(xs, RET_DK // 2, 1) * sin
            if scale is not None:
                r = r * scale
            c0 = seg * RET_Q_W + hd * RET_DK
            qk_ref[:, c0:c0 + RET_DK] = r.astype(BF16)
    base = 2 * RET_Q_W
    half = RET_V_W // 2
    for j in range(2):
        v_ref[:, j * half:(j + 1) * half] = proj(base + j * half, half).astype(BF16)
    base += RET_V_W
    for j in range(2):
        a = proj(base + j * half, half)
        rg_ref[:, j * half:(j + 1) * half] = (a * _sigmoid(a)).astype(BF16)
    base += RET_V_W
    lat_ref[...] = proj(base, LATENT_W).astype(BF16)
    base += LATENT_W
    for j in range(4):
        a = proj(base + j * half, half)
        gates_ref[:, j * half:(j + 1) * half] = _sigmoid(a).astype(BF16)


def _inproj(x2, g_mix, w_in_p, cos_r, sin_r, seq, tm):
    t = x2.shape[0]
    nseq = seq // tm
    row = lambda w: pl.BlockSpec((tm, w), lambda i: (i, 0))
    pos = pl.BlockSpec((tm, LANES), lambda i: (i % nseq, 0))
    out_w = (2 * RET_Q_W, RET_V_W, RET_V_W, LATENT_W, 2 * D_MODEL)
    return pl.pallas_call(
        _inproj_kernel,
        grid=(t // tm,),
        in_specs=[row(D_MODEL), _resident((1, D_MODEL)), _resident((D_MODEL, IN_W_PAD)), pos, pos],
        out_specs=[row(w) for w in out_w],
        out_shape=[jax.ShapeDtypeStruct((t, w), BF16) for w in out_w],
        compiler_params=_params("arbitrary"),
        name="inproj",
    )(x2, g_mix, w_in_p, cos_r, sin_r)


_T_DMAT, _T_QF, _T_QB, _T_KF, _T_KB, _T_CF, _T_CB = range(7)


def _retention_kernel(dec_ref, gn_ref, qk_ref, v_ref, rg_ref, o_ref,
                      tab_ref, sf_ref, sb_ref, sball_ref, *, ts, nblk):
    c = RET_CHUNK
    ncb = ts // c
    b = pl.program_id(0)
    ph = pl.program_id(1)
    i = pl.program_id(2)

    @pl.when((b == 0) & (ph == 0) & (i == 0))
    def _tables():
        row = lax.broadcasted_iota(jnp.int32, (c, c), 0).astype(F32)
        col = lax.broadcasted_iota(jnp.int32, (c, c), 1).astype(F32)
        diff = row - col
        for hd in range(RET_HEADS):
            def log_sigmoid(d):
                return jnp.minimum(d, 0.0) - jnp.log1p(jnp.exp(-jnp.abs(d)))
            lgf = log_sigmoid(dec_ref[0, hd:hd + 1, :])
            lgb = log_sigmoid(dec_ref[1, hd:hd + 1, :])
            tab_ref[hd, _T_DMAT] = jnp.where(diff >= 0.0,
                                             jnp.exp(lgf * jnp.maximum(diff, 0.0)),
                                             jnp.exp(lgb * jnp.maximum(-diff, 0.0)))
            tab_ref[hd, _T_QF] = jnp.exp(lgf * (row + 1.0))
            tab_ref[hd, _T_QB] = jnp.exp(lgb * (c - row))
            tab_ref[hd, _T_KF] = jnp.exp(lgf * (c - 1.0 - row))
            tab_ref[hd, _T_KB] = jnp.exp(lgb * row)
            tab_ref[hd, _T_CF] = jnp.exp(lgf * (row * 0.0 + c))
            tab_ref[hd, _T_CB] = jnp.exp(lgb * (row * 0.0 + c))

    def chunk_decay(hd, which):
        t = tab_ref[hd, which]
        return jnp.concatenate([t, t], axis=1)

    @pl.when(ph == 0)
    def _backward_states():
        @pl.when(i == 0)
        def _():
            sb_ref[...] = jnp.zeros_like(sb_ref)
        blk = nblk - 1 - i
        for cb in reversed(range(ncb)):
            r0 = cb * c
            for hd in range(RET_HEADS):
                k = qk_ref[0, r0:r0 + c, RET_Q_W + hd * RET_DK:RET_Q_W + (hd + 1) * RET_DK]
                v = v_ref[0, r0:r0 + c, hd * RET_DV:(hd + 1) * RET_DV]
                st = sb_ref[hd]
                sball_ref[blk * ncb + cb, hd] = st.astype(BF16)
                kd = (k.astype(F32) * tab_ref[hd, _T_KB]).astype(BF16)
                sb_ref[hd] = st * chunk_decay(hd, _T_CB) + _dot_tn(kd, v)

    @pl.when(ph == 1)
    def _forward():
        @pl.when(i == 0)
        def _():
            sf_ref[...] = jnp.zeros_like(sf_ref)
        for cb in range(ncb):
            r0 = cb * c
            for hd in range(RET_HEADS):
                q = qk_ref[0, r0:r0 + c, hd * RET_DK:(hd + 1) * RET_DK]
                k = qk_ref[0, r0:r0 + c, RET_Q_W + hd * RET_DK:RET_Q_W + (hd + 1) * RET_DK]
                v = v_ref[0, r0:r0 + c, hd * RET_DV:(hd + 1) * RET_DV]
                qf32 = q.astype(F32)
                kf32 = k.astype(F32)
                sd = (_dot_nt(q, k) * tab_ref[hd, _T_DMAT]).astype(BF16)
                st = sf_ref[hd]
                out = _dot(sd, v)
                out = out + _dot((qf32 * tab_ref[hd, _T_QF]).astype(BF16), st.astype(BF16))
                out = out + _dot((qf32 * tab_ref[hd, _T_QB]).astype(BF16), sball_ref[i * ncb + cb, hd])
                kd = (kf32 * tab_ref[hd, _T_KF]).astype(BF16)
                sf_ref[hd] = st * chunk_decay(hd, _T_CF) + _dot_tn(kd, v)
                mu = jnp.mean(out, axis=-1, keepdims=True)
                xc = out - mu
                var = jnp.mean(xc * xc, axis=-1, keepdims=True)
                y = xc * lax.rsqrt(var + EPS) * gn_ref[:, hd * RET_DV:(hd + 1) * RET_DV]
                gate = rg_ref[0, r0:r0 + c, hd * RET_DV:(hd + 1) * RET_DV].astype(F32)
                o_ref[0, r0:r0 + c, hd * RET_DV:(hd + 1) * RET_DV] = (y * gate).astype(BF16)


def _retention(dec, gn_g, qk, v, rg, ts):
    bsz, seq, _ = qk.shape
    nblk = seq // ts
    nchunk = seq // RET_CHUNK
    sweep = lambda b, ph, i: (b, i * ph + (nblk - 1 - i) * (1 - ph), 0)
    fwd_only = lambda b, ph, i: (b, i * ph, 0)
    return pl.pallas_call(
        functools.partial(_retention_kernel, ts=ts, nblk=nblk),
        grid=(bsz, 2, nblk),
        in_specs=[_resident((2, RET_HEADS, LANES)), _resident((1, RET_V_W)),
                  pl.BlockSpec((1, ts, 2 * RET_Q_W), sweep),
                  pl.BlockSpec((1, ts, RET_V_W), sweep),
                  pl.BlockSpec((1, ts, RET_V_W), fwd_only)],
        out_specs=pl.BlockSpec((1, ts, RET_V_W), fwd_only),
        out_shape=jax.ShapeDtypeStruct((bsz, seq, RET_V_W), BF16),
        scratch_shapes=[pltpu.VMEM((RET_HEADS, 7, RET_CHUNK, RET_CHUNK), F32),
                        pltpu.VMEM((RET_HEADS, RET_DK, RET_DV), F32),
                        pltpu.VMEM((RET_HEADS, RET_DK, RET_DV), F32),
                        pltpu.VMEM((nchunk, RET_HEADS, RET_DK, RET_DV), BF16)],
        compiler_params=_params("arbitrary", "arbitrary", "arbitrary"),
        name="retention",
    )(dec, gn_g, qk, v, rg)


def _rope64(x, cos, sin_lo, sin_hi):
    return x * cos + pltpu.roll(x, LANES - MLA_ROPE // 2, 1) * sin_lo + pltpu.roll(x, MLA_ROPE // 2, 1) * sin_hi


def _mla_prep_kernel(lat_ref, gcq_ref, wuqt_ref, gckv_ref, wk_ref, wvt_ref, gqn_ref, gkn_ref,
                     cos_ref, slo_ref, shi_ref, cost_ref, sint_ref, qt_ref, k_ref, vt_ref):
    ts = lat_ref.shape[1]

    def rms(x, g):
        ms = jnp.mean(x * x, axis=-1, keepdims=True)
        return x * lax.rsqrt(ms + EPS) * g

    cqn = rms(lat_ref[0, :, :Q_LORA].astype(F32), gcq_ref[...])
    ckvn = rms(lat_ref[0, :, Q_LORA:Q_LORA + KV_LORA].astype(F32), gckv_ref[...])
    kr = lat_ref[0, :, Q_LORA + KV_LORA:].astype(F32)
    cqn_t = cqn.T.astype(BF16)
    ckvn_t = ckvn.T.astype(BF16)

    vt_all = _dot(wvt_ref[...], ckvn_t)
    for hd in range(MLA_HEADS):
        for c in range(ts // LANES):
            vt_ref[0, hd, c] = vt_all[hd * MLA_V:(hd + 1) * MLA_V, c * LANES:(c + 1) * LANES].astype(BF16)

    half = MLA_ROPE // 2
    r0 = MLA_NOPE
    for hd in range(MLA_HEADS):
        qh = _dot(wuqt_ref[hd * MLA_QK_PAD:(hd + 1) * MLA_QK_PAD, :], cqn_t)
        for c in range(ts // LANES):
            lanes = slice(c * LANES, (c + 1) * LANES)
            qc = qh[:, lanes]
            ssq = jnp.sum(qc * qc, axis=0, keepdims=True) * (1.0 / MLA_QK)
            qn = qc * lax.rsqrt(ssq + EPS) * gqn_ref[...]
            x1 = qn[r0:r0 + half]
            x2 = qn[r0 + half:r0 + 2 * half]
            cos = cost_ref[:, lanes]
            sin = sint_ref[:, lanes]
            qt_ref[0, hd, :r0, lanes] = qn[:r0].astype(BF16)
            qt_ref[0, hd, r0:r0 + half, lanes] = (x1 * cos - x2 * sin).astype(BF16)
            qt_ref[0, hd, r0 + half:r0 + 2 * half, lanes] = (x2 * cos + x1 * sin).astype(BF16)
            qt_ref[0, hd, r0 + 2 * half:, lanes] = jnp.zeros((MLA_QK_PAD - MLA_QK, LANES), BF16)

    cos = cos_ref[...]
    slo = slo_ref[...]
    shi = shi_ref[...]
    gkn = gkn_ref[...]
    kr = _rope64(kr, cos, slo, shi)
    kr_sq = jnp.sum(kr * kr, axis=-1, keepdims=True)
    kr_rot = _rope64(kr * gkn[:, MLA_NOPE:], cos, slo, shi)
    kn_all = _dot(ckvn.astype(BF16), wk_ref[...])
    for hd in range(MLA_HEADS):
        kn = kn_all[:, hd * MLA_NOPE:(hd + 1) * MLA_NOPE]
        ssk = (jnp.sum(kn * kn, axis=-1, keepdims=True) + kr_sq) * (1.0 / MLA_QK)
        rk = lax.rsqrt(ssk + EPS)
        k_ref[0, hd, :, :MLA_NOPE] = (kn * rk * gkn[:, :MLA_NOPE]).astype(BF16)
        k_ref[0, hd, :, MLA_NOPE:] = (kr_rot * rk).astype(BF16)


def _mla_prep(lat, w, tabs, ts):
    bsz, seq, _ = lat.shape
    cos_m, slo_m, shi_m, cos_t, sin_t = tabs
    pos = pl.BlockSpec((ts, LANES), lambda b, i: (i, 0))
    pos_t = pl.BlockSpec((MLA_ROPE // 2, ts), lambda b, i: (0, i))
    return pl.pallas_call(
        _mla_prep_kernel,
        grid=(bsz, seq // ts),
        in_specs=[pl.BlockSpec((1, ts, LATENT_W), lambda b, i: (b, i, 0)),
                  _resident((1, Q_LORA)), _resident((MLA_HEADS * MLA_QK_PAD, Q_LORA)),
                  _resident((1, KV_LORA)), _resident((KV_LORA, MLA_HEADS * MLA_NOPE)),
                  _resident((MLA_HEADS * MLA_V, KV_LORA)),
                  _resident((MLA_QK_PAD, LANES)), _resident((1, MLA_QK_PAD)),
                  pos, pos, pos, pos_t, pos_t],
        out_specs=[pl.BlockSpec((1, MLA_HEADS, MLA_QK_PAD, ts), lambda b, i: (b, 0, 0, i)),
                   pl.BlockSpec((1, MLA_HEADS, ts, MLA_QK_PAD), lambda b, i: (b, 0, i, 0)),
                   pl.BlockSpec((1, MLA_HEADS, ts // LANES, MLA_V, LANES), lambda b, i: (b, 0, i, 0, 0))],
        out_shape=[jax.ShapeDtypeStruct((bsz, MLA_HEADS, MLA_QK_PAD, seq), BF16),
                   jax.ShapeDtypeStruct((bsz, MLA_HEADS, seq, MLA_QK_PAD), BF16),
                   jax.ShapeDtypeStruct((bsz, MLA_HEADS, seq // LANES, MLA_V, LANES), BF16)],
        compiler_params=_params("arbitrary", "arbitrary"),
        name="mla_prep",
    )(lat, w["g_cq"], w["w_uq_t"], w["g_ckv"], w["w_k"], w["w_v_t"], w["g_qn_rows"], w["g_kn"],
      cos_m, slo_m, shi_m, cos_t, sin_t)


ATT_UNROLL = 16
ATT_UNROLL_RM = 4
ATT_PITCH_PAD = LANES
ATT_FIXED_SHIFT_MAX = 60.0


def _attention_kernel(shift_ref, qt_ref, k_ref, vt_ref, o_ref, s0_ref, s1_ref, *, tk, nkv):
    qt = qt_ref[0, 0]
    tq = qt.shape[1]
    sub = tk // LANES
    s_refs = (s0_ref, s1_ref)

    def scores(j):
        k0 = j * tk
        if not isinstance(j, int):
            k0 = pl.multiple_of(k0, tk)
        return _dot(k_ref[0, 0, pl.ds(k0, tk), :], qt)

    def values_t(j):
        vt3 = vt_ref[0, 0, pl.ds(j * sub, sub)]
        return jnp.concatenate([vt3[i] for i in range(sub)], axis=1)

    def blocks(n_trips, unroll, trip_fn, tail_fn, carry):
        carry = lax.fori_loop(0, n_trips, trip_fn, carry)
        for j in range(n_trips * unroll, nkv):
            carry = tail_fn(j, carry)
        return carry

    shift = shift_ref[0, 0]
    fixed_ok = shift <= ATT_FIXED_SHIFT_MAX

    @pl.when(fixed_ok)
    def _fixed_shift():
        def block(j, carry):
            l, acc = carry
            e = jnp.exp2(scores(j) - shift)
            return l + jnp.sum(e, axis=0, keepdims=True), acc + _dot(values_t(j), e.astype(BF16))

        def trip(i, carry):
            for u in range(ATT_UNROLL):
                carry = block(i * ATT_UNROLL + u, carry)
            return carry

        carry = (jnp.zeros((1, tq), F32), jnp.zeros((MLA_V, tq), F32))
        l, acc = blocks(nkv // ATT_UNROLL, ATT_UNROLL, trip, block, carry)
        o_ref[0] = (acc / l).T.astype(BF16)

    @pl.when(jnp.logical_not(fixed_ok))
    def _running_max():
        def update(s_ref, j, carry):
            m, l, acc = carry
            s = s_ref[:, :tq]
            m_new = jnp.maximum(m, jnp.max(s, axis=0, keepdims=True))
            p = jnp.exp2(s - m_new)
            alpha = jnp.exp2(m - m_new)
            l = alpha * l + jnp.sum(p, axis=0, keepdims=True)
            acc = alpha * acc + _dot(values_t(j), p.astype(BF16))
            return m_new, l, acc

        def trip(i, carry):
            j = i * ATT_UNROLL_RM
            for u in range(ATT_UNROLL_RM):
                s_refs[(u + 1) % 2][:, :tq] = scores(j + u + 1)
                carry = update(s_refs[u % 2], j + u, carry)
            return carry

        def tail(j, carry):
            if j + 1 < nkv:
                s_refs[(j + 1) % 2][:, :tq] = scores(j + 1)
            return update(s_refs[j % 2], j, carry)

        carry = (jnp.full((1, tq), -jnp.inf, F32), jnp.zeros((1, tq), F32), jnp.zeros((MLA_V, tq), F32))
        s0_ref[:, :tq] = scores(0)
        _, l, acc = blocks((nkv - 1) // ATT_UNROLL_RM, ATT_UNROLL_RM, trip, tail, carry)
        o_ref[0] = (acc / l).T.astype(BF16)


def _attention(shift, qt, k, vt, tq, tk):
    bsz, nh, seq, _ = k.shape
    nkv = seq // tk
    assert nkv >= 2 and nkv % 2 == 0 and tk % LANES == 0, (seq, tk)
    return pl.pallas_call(
        functools.partial(_attention_kernel, tk=tk, nkv=nkv),
        grid=(bsz, nh, seq // tq),
        in_specs=[pl.BlockSpec(memory_space=pltpu.SMEM),
                  pl.BlockSpec((1, 1, MLA_QK_PAD, tq), lambda b, h, i: (b, h, 0, i)),
                  pl.BlockSpec((1, 1, seq, MLA_QK_PAD), lambda b, h, i: (b, h, 0, 0)),
                  pl.BlockSpec((1, 1, seq // LANES, MLA_V, LANES), lambda b, h, i: (b, h, 0, 0, 0))],
        out_specs=pl.BlockSpec((1, tq, MLA_V), lambda b, h, i: (b, i, h)),
        out_shape=jax.ShapeDtypeStruct((bsz, seq, nh * MLA_V), BF16),
        scratch_shapes=[pltpu.VMEM((tk, tq + ATT_PITCH_PAD), F32), pltpu.VMEM((tk, tq + ATT_PITCH_PAD), F32)],
        compiler_params=_params("arbitrary", "arbitrary", "arbitrary"),
        name="attention",
    )(shift, qt, k, vt)


def _merge_kernel(x_ref, ret_ref, mla_ref, gates_ref, wr_ref, wm_ref, wo_ref, o_ref):
    ret_branch = _dot(ret_ref[...], wr_ref[...])
    mla_branch = _dot(mla_ref[...], wm_ref[...])
    merged = (gates_ref[:, :D_MODEL].astype(F32) * ret_branch
              + gates_ref[:, D_MODEL:].astype(F32) * mla_branch)
    o_ref[...] = x_ref[...] + _dot(merged.astype(BF16), wo_ref[...])


def _merge(x2, ret, mla, gates, w_ret_o, w_mla_o, w_out, tm):
    t = x2.shape[0]
    row = lambda w: pl.BlockSpec((tm, w), lambda i: (i, 0))
    sq = _resident((D_MODEL, D_MODEL))
    return pl.pallas_call(
        _merge_kernel,
        grid=(t // tm,),
        in_specs=[row(D_MODEL), row(RET_V_W), row(D_MODEL), row(2 * D_MODEL), sq, sq, sq],
        out_specs=row(D_MODEL),
        out_shape=jax.ShapeDtypeStruct((t, D_MODEL), F32),
        compiler_params=_params("arbitrary"),
        name="merge",
    )(x2, ret, mla, gates, w_ret_o, w_mla_o, w_out)


FFN_HALO = 8
FFN_NC = 256
FFN_U_BUFS = 4

def _ffn_kernel(x_ref, prev_ref, next_ref, g_ref, wup_ref, cw_ref, cb_ref, wdn_ref, o_ref, h_ref, u_ref,
                act_ref, *, ts, nblk):
    i = pl.program_id(1)
    g = g_ref[...]

    def rms(x):
        ms = jnp.mean(x * x, axis=-1, keepdims=True)
        return x * lax.rsqrt(ms + EPS) * g

    keep_prev = jnp.where(i > 0, 1.0, 0.0)
    keep_next = jnp.where(i < nblk - 1, 1.0, 0.0)
    xm = x_ref[0]
    h_ref[:FFN_HALO] = (rms(prev_ref[0]) * keep_prev).astype(BF16)
    h_ref[FFN_HALO:FFN_HALO + ts] = rms(xm).astype(BF16)
    h_ref[FFN_HALO + ts:] = (rms(next_ref[0]) * keep_next).astype(BF16)
    h = h_ref[...]

    def conv(slot, c0, l0):
        w = cw_ref[:, c0 + l0:c0 + l0 + LANES]
        lanes = pl.ds(l0, LANES)
        return (u_ref[slot, pl.ds(FFN_HALO - 1, ts), lanes] * w[0:1]
                + u_ref[slot, pl.ds(FFN_HALO, ts), lanes] * w[1:2]
                + u_ref[slot, pl.ds(FFN_HALO + 1, ts), lanes] * w[2:3]
                + cb_ref[:, c0 + l0:c0 + l0 + LANES])

    for n in range(D_FF // FFN_NC):
        ca = n * FFN_NC
        sa = (2 * n) % FFN_U_BUFS
        sb = (2 * n + 1) % FFN_U_BUFS
        u_ref[sa] = _dot(h, wup_ref[:, ca:ca + FFN_NC])
        u_ref[sb] = _dot(h, wup_ref[:, D_FF + ca:D_FF + ca + FFN_NC])
        for l0 in range(0, FFN_NC, LANES):
            ua = conv(sa, ca, l0)
            ub = conv(sb, D_FF + ca, l0)
            act_ref[:, ca + l0:ca + l0 + LANES] = (ua * _sigmoid(ua) * ub).astype(BF16)
    o_ref[0] = xm + _dot(act_ref[...], wdn_ref[...])


def _ffn(x1, g_ffn, w_up, conv_w, conv_b, w_down, ts):
    bsz, seq, _ = x1.shape
    nblk = seq // ts
    hb = ts // FFN_HALO
    last_halo = seq // FFN_HALO - 1
    return pl.pallas_call(
        functools.partial(_ffn_kernel, ts=ts, nblk=nblk),
        grid=(bsz, nblk),
        in_specs=[pl.BlockSpec((1, ts, D_MODEL), lambda b, i: (b, i, 0)),
                  pl.BlockSpec((1, FFN_HALO, D_MODEL), lambda b, i: (b, jnp.maximum(i * hb - 1, 0), 0)),
                  pl.BlockSpec((1, FFN_HALO, D_MODEL), lambda b, i: (b, jnp.minimum((i + 1) * hb, last_halo), 0)),
                  _resident((1, D_MODEL)), _resident((D_MODEL, 2 * D_FF)),
                  _resident((3, 2 * D_FF)), _resident((1, 2 * D_FF)), _resident((D_FF, D_MODEL))],
        out_specs=pl.BlockSpec((1, ts, D_MODEL), lambda b, i: (b, i, 0)),
        out_shape=jax.ShapeDtypeStruct((bsz, seq, D_MODEL), F32),
        scratch_shapes=[pltpu.VMEM((ts + 2 * FFN_HALO, D_MODEL), BF16),
                        pltpu.VMEM((FFN_U_BUFS, ts + 2 * FFN_HALO, FFN_NC), F32),
                        pltpu.VMEM((ts, D_FF), BF16)],
        compiler_params=_params("arbitrary", "arbitrary"),
        name="ffn",
    )(x1, x1, x1, g_ffn, w_up, conv_w, conv_b, w_down)


def _rope_tables(seq):
    pos = jnp.arange(seq).astype(F32)[:, None]

    def cos_sin(d):
        inv = ROPE_BASE ** (-jnp.arange(0, d, 2, dtype=F32) / d)
        ang = pos * inv[None, :]
        return jnp.cos(ang), jnp.sin(ang)

    cr, sr = cos_sin(RET_DK)
    cm, sm = cos_sin(MLA_ROPE)
    zh = jnp.zeros_like(sm)
    zpad = jnp.zeros((seq, LANES - MLA_ROPE), F32)
    ret = (jnp.concatenate([cr, cr], 1), jnp.concatenate([-sr, sr], 1))
    mla = (jnp.concatenate([cm, cm, zpad], 1), jnp.concatenate([-sm, zh, zpad], 1),
           jnp.concatenate([zh, sm, zpad], 1), cm.T, sm.T)
    return ret, mla


def _tile(seq, want):
    t = min(seq, want)
    assert seq % t == 0, (seq, t)
    return t


def _trunk(x, p, depth):
    bsz, seq, _ = x.shape
    t = bsz * seq
    (cos_r, sin_r), mla_tabs = _rope_tables(seq)
    tm = _tile(seq, 512)
    for l in range(depth):
        w = {k: v[l] for k, v in p.items()}
        x2 = x.reshape(t, D_MODEL)
        qk, rv, rg, lat, gates = _inproj(x2, w["g_mix"], w["w_in"], cos_r, sin_r, seq, tm)
        ret = _retention(w["dec"], w["ret_gn_g"],
                         qk.reshape(bsz, seq, -1), rv.reshape(bsz, seq, -1), rg.reshape(bsz, seq, -1),
                         _tile(seq, 512))
        q, k, v = _mla_prep(lat.reshape(bsz, seq, -1), w, mla_tabs, _tile(seq, 512))
        mla = _attention(w["att_shift"], q, k, v, _tile(seq, 1024), _tile(seq, 512))
        x1 = _merge(x2, ret.reshape(t, -1), mla.reshape(t, -1), gates,
                    w["w_ret_o"], w["w_mla_o"], w["w_out"], tm)
        x = _ffn(x1.reshape(bsz, seq, D_MODEL), w["g_ffn"], w["w_up"], w["conv_w"], w["conv_b"],
                 w["w_down"], _tile(seq, 512))
    return x


def _prepare_weights(g_mix, w_in, ret_decay_fwd, ret_decay_bwd, ret_gn_g, w_ret_o, g_cq, w_uq, g_ckv,
                     w_ukv, g_qn, g_kn, w_mla_o, w_out, g_ffn, w_up, conv_w, conv_b, w_down):
    depth = w_in.shape[0]
    kr_end = 2 * RET_Q_W + 2 * RET_V_W + Q_LORA + KV_LORA + MLA_ROPE
    w_in_p = jnp.concatenate(
        [w_in[:, :, :kr_end], jnp.zeros((depth, D_MODEL, LANES - MLA_ROPE), w_in.dtype), w_in[:, :, kr_end:]],
        axis=2).astype(BF16)
    pad_h = MLA_QK_PAD - MLA_QK
    w_uq_p = jnp.pad(w_uq.reshape(depth, Q_LORA, MLA_HEADS, MLA_QK), ((0, 0), (0, 0), (0, 0), (0, pad_h)))
    w_uq_t = jnp.swapaxes(w_uq_p.reshape(depth, Q_LORA, MLA_HEADS * MLA_QK_PAD), 1, 2).astype(BF16)
    w_ukv_h = w_ukv.reshape(depth, KV_LORA, MLA_HEADS, MLA_NOPE + MLA_V)
    w_k = w_ukv_h[..., :MLA_NOPE].reshape(depth, KV_LORA, MLA_HEADS * MLA_NOPE).astype(BF16)
    w_v_t = jnp.swapaxes(w_ukv_h[..., MLA_NOPE:].reshape(depth, KV_LORA, MLA_HEADS * MLA_V), 1, 2).astype(BF16)
    qscale = LOG2_E * MLA_QK ** -0.5
    g_qn_rows = jnp.broadcast_to((jnp.pad(g_qn, ((0, 0), (0, pad_h))) * qscale)[..., None],
                                 (depth, MLA_QK_PAD, LANES)).astype(F32)
    att_shift = (MLA_QK * qscale * jnp.max(jnp.abs(g_qn), axis=1) * jnp.max(jnp.abs(g_kn), axis=1))
    att_shift = att_shift.astype(F32).reshape(depth, 1, 1)
    dec = jnp.stack([ret_decay_fwd, ret_decay_bwd], axis=1).astype(F32)
    dec = jnp.broadcast_to(dec[..., None], (depth, 2, RET_HEADS, LANES))
    row = lambda a: a[:, None, :].astype(F32)
    return {
        "g_mix": row(g_mix), "w_in": w_in_p, "dec": dec, "ret_gn_g": row(ret_gn_g),
        "w_ret_o": w_ret_o.astype(BF16), "g_cq": row(g_cq), "w_uq_t": w_uq_t, "g_ckv": row(g_ckv),
        "w_k": w_k, "w_v_t": w_v_t, "att_shift": att_shift,
        "g_qn_rows": g_qn_rows, "g_kn": row(jnp.pad(g_kn, ((0, 0), (0, pad_h)))),
        "w_mla_o": w_mla_o.astype(BF16), "w_out": w_out.astype(BF16), "g_ffn": row(g_ffn),
        "w_up": w_up.astype(BF16), "conv_w": conv_w.astype(F32), "conv_b": row(conv_b),
        "w_down": w_down.astype(BF16),
    }


def kernel(x_prompt, x_sample, g_mix, w_in, ret_decay_fwd, ret_decay_bwd, ret_gn_g, w_ret_o, g_cq, w_uq,
           g_ckv, w_ukv, g_qn, g_kn, w_mla_o, w_out, g_ffn, w_up, conv_w, conv_b, w_down):
    depth = w_in.shape[0]
    p = _prepare_weights(g_mix, w_in, ret_decay_fwd, ret_decay_bwd, ret_gn_g, w_ret_o, g_cq, w_uq, g_ckv,
                         w_ukv, g_qn, g_kn, w_mla_o, w_out, g_ffn, w_up, conv_w, conv_b, w_down)
    return (_trunk(x_prompt, p, depth), _trunk(x_sample, p, depth))
```

```python
import functools

import jax
import jax.numpy as jnp
from jax import lax
from jax.experimental import pallas as pl
from jax.experimental.pallas import tpu as pltpu

D_MODEL = 1024
RET_HEADS = 4
RET_DK = 128
RET_DV = 256
RET_CHUNK = 128
MLA_HEADS = 8
MLA_NOPE = 128
MLA_ROPE = 64
MLA_V = 128
MLA_QK = MLA_NOPE + MLA_ROPE
MLA_QK_PAD = 256
Q_LORA = 384
KV_LORA = 256
D_FF = 2816
ROPE_BASE = 10000.0
EPS = 1e-6
LOG2_E = 1.4426950408889634

RET_Q_W = RET_HEADS * RET_DK
RET_V_W = RET_HEADS * RET_DV
LATENT_W = Q_LORA + KV_LORA + 128
IN_W_PAD = 2 * RET_Q_W + 2 * RET_V_W + LATENT_W + 2 * D_MODEL

LANES = 128
VMEM_LIMIT_BYTES = 56 * 1024 * 1024

F32 = jnp.float32
BF16 = jnp.bfloat16


def _params(*semantics):
    return pltpu.CompilerParams(dimension_semantics=semantics, vmem_limit_bytes=VMEM_LIMIT_BYTES)


def _resident(shape):
    return pl.BlockSpec(shape, lambda *_: (0,) * len(shape), pipeline_mode=pl.Buffered(1))


def _sigmoid(x):
    return 1.0 / (1.0 + jnp.exp(-x))


def _dot(a, b):
    return jnp.dot(a, b, preferred_element_type=F32)


def _dot_nt(a, b):
    return lax.dot_general(a, b, (((1,), (1,)), ((), ())), preferred_element_type=F32)


def _dot_tn(a, b):
    return lax.dot_general(a, b, (((0,), (0,)), ((), ())), preferred_element_type=F32)


def _inproj_kernel(x_ref, g_ref, w_ref, cos_ref, sin_ref,
                   qk_ref, v_ref, rg_ref, lat_ref, gates_ref):
    xf = x_ref[...]
    ms = jnp.mean(xf * xf, axis=-1, keepdims=True)
    h = (xf * lax.rsqrt(ms + EPS) * g_ref[...]).astype(BF16)
    cos = cos_ref[...]
    sin = sin_ref[...]

    def proj(c0, width):
        return _dot(h, w_ref[:, c0:c0 + width])

    for seg, scale in ((0, None), (1, RET_DK ** -0.5)):
        acc = proj(seg * RET_Q_W, RET_Q_W)
        for hd in range(RET_HEADS):
            xs = acc[:, hd * RET_DK:(hd + 1) * RET_DK]
            r = xs * cos + pltpu.roll(xs, RET_DK // 2, 1) * sin
            if scale is not None:
                r = r * scale
            c0 = seg * RET_Q_W + hd * RET_DK
            qk_ref[:, c0:c0 + RET_DK] = r.astype(BF16)
    base = 2 * RET_Q_W
    half = RET_V_W // 2
    for j in range(2):
        v_ref[:, j * half:(j + 1) * half] = proj(base + j * half, half).astype(BF16)
    base += RET_V_W
    for j in range(2):
        a = proj(base + j * half, half)
        rg_ref[:, j * half:(j + 1) * half] = (a * _sigmoid(a)).astype(BF16)
    base += RET_V_W
    lat_ref[...] = proj(base, LATENT_W).astype(BF16)
    base += LATENT_W
    for j in range(4):
        a = proj(base + j * half, half)
        gates_ref[:, j * half:(j + 1) * half] = _sigmoid(a).astype(BF16)


def _inproj(x2, g_mix, w_in_p, cos_r, sin_r, seq, tm):
    t = x2.shape[0]
    nseq = seq // tm
    row = lambda w: pl.BlockSpec((tm, w), lambda i: (i, 0))
    pos = pl.BlockSpec((tm, LANES), lambda i: (i % nseq, 0))
    out_w = (2 * RET_Q_W, RET_V_W, RET_V_W, LATENT_W, 2 * D_MODEL)
    return pl.pallas_call(
        _inproj_kernel,
        grid=(t // tm,),
        in_specs=[row(D_MODEL), _resident((1, D_MODEL)), _resident((D_MODEL, IN_W_PAD)), pos, pos],
        out_specs=[row(w) for w in out_w],
        out_shape=[jax.ShapeDtypeStruct((t, w), BF16) for w in out_w],
        compiler_params=_params("arbitrary"),
        name="inproj",
    )(x2, g_mix, w_in_p, cos_r, sin_r)


_T_DMAT, _T_QF, _T_QB, _T_KF, _T_KB, _T_CF, _T_CB = range(7)


def _retention_kernel(dec_ref, gn_ref, qk_ref, v_ref, rg_ref, o_ref,
                      tab_ref, sf_ref, sb_ref, sball_ref, *, ts, nblk):
    c = RET_CHUNK
    ncb = ts // c
    b = pl.program_id(0)
    ph = pl.program_id(1)
    i = pl.program_id(2)

    @pl.when((b == 0) & (ph == 0) & (i == 0))
    def _tables():
        row = lax.broadcasted_iota(jnp.int32, (c, c), 0).astype(F32)
        col = lax.broadcasted_iota(jnp.int32, (c, c), 1).astype(F32)
        diff = row - col
        for hd in range(RET_HEADS):
            def log_sigmoid(d):
                return jnp.minimum(d, 0.0) - jnp.log1p(jnp.exp(-jnp.abs(d)))
            lgf = log_sigmoid(dec_ref[0, hd:hd + 1, :])
            lgb = log_sigmoid(dec_ref[1, hd:hd + 1, :])
            tab_ref[hd, _T_DMAT] = jnp.where(diff >= 0.0,
                                             jnp.exp(lgf * jnp.maximum(diff, 0.0)),
                                             jnp.exp(lgb * jnp.maximum(-diff, 0.0)))
            tab_ref[hd, _T_QF] = jnp.exp(lgf * (row + 1.0))
            tab_ref[hd, _T_QB] = jnp.exp(lgb * (c - row))
            tab_ref[hd, _T_KF] = jnp.exp(lgf * (c - 1.0 - row))
            tab_ref[hd, _T_KB] = jnp.exp(lgb * row)
            tab_ref[hd, _T_CF] = jnp.exp(lgf * (row * 0.0 + c))
            tab_ref[hd, _T_CB] = jnp.exp(lgb * (row * 0.0 + c))

    def chunk_decay(hd, which):
        t = tab_ref[hd, which]
        return jnp.concatenate([t, t], axis=1)

    @pl.when(ph == 0)
    def _backward_states():
        @pl.when(i == 0)
        def _():
            sb_ref[...] = jnp.zeros_like(sb_ref)
        blk = nblk - 1 - i
        for cb in reversed(range(ncb)):
            r0 = cb * c
            for hd in range(RET_HEADS):
                k = qk_ref[0, r0:r0 + c, RET_Q_W + hd * RET_DK:RET_Q_W + (hd + 1) * RET_DK]
                v = v_ref[0, r0:r0 + c, hd * RET_DV:(hd + 1) * RET_DV]
                st = sb_ref[hd]
                sball_ref[blk * ncb + cb, hd] = st.astype(BF16)
                kd = (k.astype(F32) * tab_ref[hd, _T_KB]).astype(BF16)
                sb_ref[hd] = st * chunk_decay(hd, _T_CB) + _dot_tn(kd, v)

    @pl.when(ph == 1)
    def _forward():
        @pl.when(i == 0)
        def _():
            sf_ref[...] = jnp.zeros_like(sf_ref)
        for cb in range(ncb):
            r0 = cb * c
            for hd in range(RET_HEADS):
                q = qk_ref[0, r0:r0 + c, hd * RET_DK:(hd + 1) * RET_DK]
                k = qk_ref[0, r0:r0 + c, RET_Q_W + hd * RET_DK:RET_Q_W + (hd + 1) * RET_DK]
                v = v_ref[0, r0:r0 + c, hd * RET_DV:(hd + 1) * RET_DV]
                qf32 = q.astype(F32)
                kf32 = k.astype(F32)
                sd = (_dot_nt(q, k) * tab_ref[hd, _T_DMAT]).astype(BF16)
                st = sf_ref[hd]
                out = _dot(sd, v)
                out = out + _dot((qf32 * tab_ref[hd, _T_QF]).astype(BF16), st.astype(BF16))
                out = out + _dot((qf32 * tab_ref[hd, _T_QB]).astype(BF16), sball_ref[i * ncb + cb, hd])
                kd = (kf32 * tab_ref[hd, _T_KF]).astype(BF16)
                sf_ref[hd] = st * chunk_decay(hd, _T_CF) + _dot_tn(kd, v)
                mu = jnp.mean(out, axis=-1, keepdims=True)
                xc = out - mu
                var = jnp.mean(xc * xc, axis=-1, keepdims=True)
                y = xc * lax.rsqrt(var + EPS) * gn_ref[:, hd * RET_DV:(hd + 1) * RET_DV]
                gate = rg_ref[0, r0:r0 + c, hd * RET_DV:(hd + 1) * RET_DV].astype(F32)
                o_ref[0, r0:r0 + c, hd * RET_DV:(hd + 1) * RET_DV] = (y * gate).astype(BF16)


def _retention(dec, gn_g, qk, v, rg, ts):
    bsz, seq, _ = qk.shape
    nblk = seq // ts
    nchunk = seq // RET_CHUNK
    sweep = lambda b, ph, i: (b, i * ph + (nblk - 1 - i) * (1 - ph), 0)
    fwd_only = lambda b, ph, i: (b, i * ph, 0)
    return pl.pallas_call(
        functools.partial(_retention_kernel, ts=ts, nblk=nblk),
        grid=(bsz, 2, nblk),
        in_specs=[_resident((2, RET_HEADS, LANES)), _resident((1, RET_V_W)),
                  pl.BlockSpec((1, ts, 2 * RET_Q_W), sweep),
                  pl.BlockSpec((1, ts, RET_V_W), sweep),
                  pl.BlockSpec((1, ts, RET_V_W), fwd_only)],
        out_specs=pl.BlockSpec((1, ts, RET_V_W), fwd_only),
        out_shape=jax.ShapeDtypeStruct((bsz, seq, RET_V_W), BF16),
        scratch_shapes=[pltpu.VMEM((RET_HEADS, 7, RET_CHUNK, RET_CHUNK), F32),
                        pltpu.VMEM((RET_HEADS, RET_DK, RET_DV), F32),
                        pltpu.VMEM((RET_HEADS, RET_DK, RET_DV), F32),
                        pltpu.VMEM((nchunk, RET_HEADS, RET_DK, RET_DV), BF16)],
        compiler_params=_params("arbitrary", "arbitrary", "arbitrary"),
        name="retention",
    )(dec, gn_g, qk, v, rg)


def _rope64(x, cos, sin_lo, sin_hi):
    return x * cos + pltpu.roll(x, LANES - MLA_ROPE // 2, 1) * sin_lo + pltpu.roll(x, MLA_ROPE // 2, 1) * sin_hi


def _mla_prep_kernel(lat_ref, gcq_ref, wuqt_ref, gckv_ref, wk_ref, wvt_ref, gqn_ref, gkn_ref,
                     cos_ref, slo_ref, shi_ref, cost_ref, sint_ref, qt_ref, k_ref, vt_ref):
    ts = lat_ref.shape[1]

    def rms(x, g):
        ms = jnp.mean(x * x, axis=-1, keepdims=True)
        return x * lax.rsqrt(ms + EPS) * g

    cqn = rms(lat_ref[0, :, :Q_LORA].astype(F32), gcq_ref[...])
    ckvn = rms(lat_ref[0, :, Q_LORA:Q_LORA + KV_LORA].astype(F32), gckv_ref[...])
    kr = lat_ref[0, :, Q_LORA + KV_LORA:].astype(F32)
    cqn_t = cqn.T.astype(BF16)
    ckvn_t = ckvn.T.astype(BF16)

    vt_all = _dot(wvt_ref[...], ckvn_t)
    for hd in range(MLA_HEADS):
        for c in range(ts // LANES):
            vt_ref[0, hd, c] = vt_all[hd * MLA_V:(hd + 1) * MLA_V, c * LANES:(c + 1) * LANES].astype(BF16)

    half = MLA_ROPE // 2
    r0 = MLA_NOPE
    for hd in range(MLA_HEADS):
        qh = _dot(wuqt_ref[hd * MLA_QK_PAD:(hd + 1) * MLA_QK_PAD, :], cqn_t)
        for c in range(ts // LANES):
            lanes = slice(c * LANES, (c + 1) * LANES)
            qc = qh[:, lanes]
            ssq = jnp.sum(qc * qc, axis=0, keepdims=True) * (1.0 / MLA_QK)
            qn = qc * lax.rsqrt(ssq + EPS) * gqn_ref[...]
            x1 = qn[r0:r0 + half]
            x2 = qn[r0 + half:r0 + 2 * half]
            cos = cost_ref[:, lanes]
            sin = sint_ref[:, lanes]
            qt_ref[0, hd, :r0, lanes] = qn[:r0].astype(BF16)
            qt_ref[0, hd, r0:r0 + half, lanes] = (x1 * cos - x2 * sin).astype(BF16)
            qt_ref[0, hd, r0 + half:r0 + 2 * half, lanes] = (x2 * cos + x1 * sin).astype(BF16)
            qt_ref[0, hd, r0 + 2 * half:, lanes] = jnp.zeros((MLA_QK_PAD - MLA_QK, LANES), BF16)

    cos = cos_ref[...]
    slo = slo_ref[...]
    shi = shi_ref[...]
    gkn = gkn_ref[...]
    kr = _rope64(kr, cos, slo, shi)
    kr_sq = jnp.sum(kr * kr, axis=-1, keepdims=True)
    kr_rot = _rope64(kr * gkn[:, MLA_NOPE:], cos, slo, shi)
    kn_all = _dot(ckvn.astype(BF16), wk_ref[...])
    for hd in range(MLA_HEADS):
        kn = kn_all[:, hd * MLA_NOPE:(hd + 1) * MLA_NOPE]
        ssk = (jnp.sum(kn * kn, axis=-1, keepdims=True) + kr_sq) * (1.0 / MLA_QK)
        rk = lax.rsqrt(ssk + EPS)
        k_ref[0, hd, :, :MLA_NOPE] = (kn * rk * gkn[:, :MLA_NOPE]).astype(BF16)
        k_ref[0, hd, :, MLA_NOPE:] = (kr_rot * rk).astype(BF16)


def _mla_prep(lat, w, tabs, ts):
    bsz, seq, _ = lat.shape
    cos_m, slo_m, shi_m, cos_t, sin_t = tabs
    pos = pl.BlockSpec((ts, LANES), lambda b, i: (i, 0))
    pos_t = pl.BlockSpec((MLA_ROPE // 2, ts), lambda b, i: (0, i))
    return pl.pallas_call(
        _mla_prep_kernel,
        grid=(bsz, seq // ts),
        in_specs=[pl.BlockSpec((1, ts, LATENT_W), lambda b, i: (b, i, 0)),
                  _resident((1, Q_LORA)), _resident((MLA_HEADS * MLA_QK_PAD, Q_LORA)),
                  _resident((1, KV_LORA)), _resident((KV_LORA, MLA_HEADS * MLA_NOPE)),
                  _resident((MLA_HEADS * MLA_V, KV_LORA)),
                  _resident((MLA_QK_PAD, LANES)), _resident((1, MLA_QK_PAD)),
                  pos, pos, pos, pos_t, pos_t],
        out_specs=[pl.BlockSpec((1, MLA_HEADS, MLA_QK_PAD, ts), lambda b, i: (b, 0, 0, i)),
                   pl.BlockSpec((1, MLA_HEADS, ts, MLA_QK_PAD), lambda b, i: (b, 0, i, 0)),
                   pl.BlockSpec((1, MLA_HEADS, ts // LANES, MLA_V, LANES), lambda b, i: (b, 0, i, 0, 0))],
        out_shape=[jax.ShapeDtypeStruct((bsz, MLA_HEADS, MLA_QK_PAD, seq), BF16),
                   jax.ShapeDtypeStruct((bsz, MLA_HEADS, seq, MLA_QK_PAD), BF16),
                   jax.ShapeDtypeStruct((bsz, MLA_HEADS, seq // LANES, MLA_V, LANES), BF16)],
        compiler_params=_params("arbitrary", "arbitrary"),
        name="mla_prep",
    )(lat, w["g_cq"], w["w_uq_t"], w["g_ckv"], w["w_k"], w["w_v_t"], w["g_qn_rows"], w["g_kn"],
      cos_m, slo_m, shi_m, cos_t, sin_t)


ATT_UNROLL = 16
ATT_UNROLL_RM = 4
ATT_PITCH_PAD = LANES
ATT_FIXED_SHIFT_MAX = 60.0


def _attention_kernel(shift_ref, qt_ref, k_ref, vt_ref, o_ref, s0_ref, s1_ref, *, tk, nkv):
    qt = qt_ref[0, 0]
    tq = qt.shape[1]
    sub = tk // LANES
    s_refs = (s0_ref, s1_ref)

    def scores(j):
        k0 = j * tk
        if not isinstance(j, int):
            k0 = pl.multiple_of(k0, tk)
        return _dot(k_ref[0, 0, pl.ds(k0, tk), :], qt)

    def values_t(j):
        vt3 = vt_ref[0, 0, pl.ds(j * sub, sub)]
        return jnp.concatenate([vt3[i] for i in range(sub)], axis=1)

    def blocks(n_trips, unroll, trip_fn, tail_fn, carry):
        carry = lax.fori_loop(0, n_trips, trip_fn, carry)
        for j in range(n_trips * unroll, nkv):
            carry = tail_fn(j, carry)
        return carry

    shift = shift_ref[0, 0]
    fixed_ok = shift <= ATT_FIXED_SHIFT_MAX

    @pl.when(fixed_ok)
    def _fixed_shift():
        def block(j, carry):
            l, acc = carry
            e = jnp.exp2(scores(j) - shift)
            return l + jnp.sum(e, axis=0, keepdims=True), acc + _dot(values_t(j), e.astype(BF16))

        def trip(i, carry):
            for u in range(ATT_UNROLL):
                carry = block(i * ATT_UNROLL + u, carry)
            return carry

        carry = (jnp.zeros((1, tq), F32), jnp.zeros((MLA_V, tq), F32))
        l, acc = blocks(nkv // ATT_UNROLL, ATT_UNROLL, trip, block, carry)
        o_ref[0] = (acc / l).T.astype(BF16)

    @pl.when(jnp.logical_not(fixed_ok))
    def _running_max():
        def update(s_ref, j, carry):
            m, l, acc = carry
            s = s_ref[:, :tq]
            m_new = jnp.maximum(m, jnp.max(s, axis=0, keepdims=True))
            p = jnp.exp2(s - m_new)
            alpha = jnp.exp2(m - m_new)
            l = alpha * l + jnp.sum(p, axis=0, keepdims=True)
            acc = alpha * acc + _dot(values_t(j), p.astype(BF16))
            return m_new, l, acc

        def trip(i, carry):
            j = i * ATT_UNROLL_RM
            for u in range(ATT_UNROLL_RM):
                s_refs[(u + 1) % 2][:, :tq] = scores(j + u + 1)
                carry = update(s_refs[u % 2], j + u, carry)
            return carry

        def tail(j, carry):
            if j + 1 < nkv:
                s_refs[(j + 1) % 2][:, :tq] = scores(j + 1)
            return update(s_refs[j % 2], j, carry)

        carry = (jnp.full((1, tq), -jnp.inf, F32), jnp.zeros((1, tq), F32), jnp.zeros((MLA_V, tq), F32))
        s0_ref[:, :tq] = scores(0)
        _, l, acc = blocks((nkv - 1) // ATT_UNROLL_RM, ATT_UNROLL_RM, trip, tail, carry)
        o_ref[0] = (acc / l).T.astype(BF16)


def _attention(shift, qt, k, vt, tq, tk):
    bsz, nh, seq, _ = k.shape
    nkv = seq // tk
    assert nkv >= 2 and nkv % 2 == 0 and tk % LANES == 0, (seq, tk)
    return pl.pallas_call(
        functools.partial(_attention_kernel, tk=tk, nkv=nkv),
        grid=(bsz, nh, seq // tq),
        in_specs=[pl.BlockSpec(memory_space=pltpu.SMEM),
                  pl.BlockSpec((1, 1, MLA_QK_PAD, tq), lambda b, h, i: (b, h, 0, i)),
                  pl.BlockSpec((1, 1, seq, MLA_QK_PAD), lambda b, h, i: (b, h, 0, 0)),
                  pl.BlockSpec((1, 1, seq // LANES, MLA_V, LANES), lambda b, h, i: (b, h, 0, 0, 0))],
        out_specs=pl.BlockSpec((1, tq, MLA_V), lambda b, h, i: (b, i, h)),
        out_shape=jax.ShapeDtypeStruct((bsz, seq, nh * MLA_V), BF16),
        scratch_shapes=[pltpu.VMEM((tk, tq + ATT_PITCH_PAD), F32), pltpu.VMEM((tk, tq + ATT_PITCH_PAD), F32)],
        compiler_params=_params("arbitrary", "arbitrary", "arbitrary"),
        name="attention",
    )(shift, qt, k, vt)


def _merge_kernel(x_ref, ret_ref, mla_ref, gates_ref, wr_ref, wm_ref, wo_ref, o_ref):
    ret_branch = _dot(ret_ref[...], wr_ref[...])
    mla_branch = _dot(mla_ref[...], wm_ref[...])
    merged = (gates_ref[:, :D_MODEL].astype(F32) * ret_branch
              + gates_ref[:, D_MODEL:].astype(F32) * mla_branch)
    o_ref[...] = x_ref[...] + _dot(merged.astype(BF16), wo_ref[...])


def _merge(x2, ret, mla, gates, w_ret_o, w_mla_o, w_out, tm):
    t = x2.shape[0]
    row = lambda w: pl.BlockSpec((tm, w), lambda i: (i, 0))
    sq = _resident((D_MODEL, D_MODEL))
    return pl.pallas_call(
        _merge_kernel,
        grid=(t // tm,),
        in_specs=[row(D_MODEL), row(RET_V_W), row(D_MODEL), row(2 * D_MODEL), sq, sq, sq],
        out_specs=row(D_MODEL),
        out_shape=jax.ShapeDtypeStruct((t, D_MODEL), F32),
        compiler_params=_params("arbitrary"),
        name="merge",
    )(x2, ret, mla, gates, w_ret_o, w_mla_o, w_out)


FFN_HALO = 8
FFN_NC = 256
FFN_U_BUFS = 4

def _ffn_kernel(x_ref, prev_ref, next_ref, g_ref, wup_ref, cw_ref, cb_ref, wdn_ref, o_ref, h_ref, u_ref,
                act_ref, *, ts, nblk):
    i = pl.program_id(1)
    g = g_ref[...]

    def rms(x):
        ms = jnp.mean(x * x, axis=-1, keepdims=True)
        return x * lax.rsqrt(ms + EPS) * g

    keep_prev = jnp.where(i > 0, 1.0, 0.0)
    keep_next = jnp.where(i < nblk - 1, 1.0, 0.0)
    xm = x_ref[0]
    h_ref[:FFN_HALO] = (rms(prev_ref[0]) * keep_prev).astype(BF16)
    h_ref[FFN_HALO:FFN_HALO + ts] = rms(xm).astype(BF16)
    h_ref[FFN_HALO + ts:] = (rms(next_ref[0]) * keep_next).astype(BF16)
    h = h_ref[...]

    def conv(slot, c0):
        u_ref[slot] = _dot(h, wup_ref[:, c0:c0 + FFN_NC])
        w = cw_ref[:, c0:c0 + FFN_NC]
        return (u_ref[slot, pl.ds(FFN_HALO - 1, ts), :] * w[0:1]
                + u_ref[slot, pl.ds(FFN_HALO, ts), :] * w[1:2]
                + u_ref[slot, pl.ds(FFN_HALO + 1, ts), :] * w[2:3]
                + cb_ref[:, c0:c0 + FFN_NC])

    for n in range(D_FF // FFN_NC):
        ca = n * FFN_NC
        ua = conv((2 * n) % FFN_U_BUFS, ca)
        ub = conv((2 * n + 1) % FFN_U_BUFS, D_FF + ca)
        act_ref[:, ca:ca + FFN_NC] = (ua * _sigmoid(ua) * ub).astype(BF16)
    o_ref[0] = xm + _dot(act_ref[...], wdn_ref[...])


def _ffn(x1, g_ffn, w_up, conv_w, conv_b, w_down, ts):
    bsz, seq, _ = x1.shape
    nblk = seq // ts
    hb = ts // FFN_HALO
    last_halo = seq // FFN_HALO - 1
    return pl.pallas_call(
        functools.partial(_ffn_kernel, ts=ts, nblk=nblk),
        grid=(bsz, nblk),
        in_specs=[pl.BlockSpec((1, ts, D_MODEL), lambda b, i: (b, i, 0)),
                  pl.BlockSpec((1, FFN_HALO, D_MODEL), lambda b, i: (b, jnp.maximum(i * hb - 1, 0), 0)),
                  pl.BlockSpec((1, FFN_HALO, D_MODEL), lambda b, i: (b, jnp.minimum((i + 1) * hb, last_halo), 0)),
                  _resident((1, D_MODEL)), _resident((D_MODEL, 2 * D_FF)),
                  _resident((3, 2 * D_FF)), _resident((1, 2 * D_FF)), _resident((D_FF, D_MODEL))],
        out_specs=pl.BlockSpec((1, ts, D_MODEL), lambda b, i: (b, i, 0)),
        out_shape=jax.ShapeDtypeStruct((bsz, seq, D_MODEL), F32),
        scratch_shapes=[pltpu.VMEM((ts + 2 * FFN_HALO, D_MODEL), BF16),
                        pltpu.VMEM((FFN_U_BUFS, ts + 2 * FFN_HALO, FFN_NC), F32),
                        pltpu.VMEM((ts, D_FF), BF16)],
        compiler_params=_params("arbitrary", "arbitrary"),
        name="ffn",
    )(x1, x1, x1, g_ffn, w_up, conv_w, conv_b, w_down)


def _rope_tables(seq):
    pos = jnp.arange(seq).astype(F32)[:, None]

    def cos_sin(d):
        inv = ROPE_BASE ** (-jnp.arange(0, d, 2, dtype=F32) / d)
        ang = pos * inv[None, :]
        return jnp.cos(ang), jnp.sin(ang)

    cr, sr = cos_sin(RET_DK)
    cm, sm = cos_sin(MLA_ROPE)
    zh = jnp.zeros_like(sm)
    zpad = jnp.zeros((seq, LANES - MLA_ROPE), F32)
    ret = (jnp.concatenate([cr, cr], 1), jnp.concatenate([-sr, sr], 1))
    mla = (jnp.concatenate([cm, cm, zpad], 1), jnp.concatenate([-sm, zh, zpad], 1),
           jnp.concatenate([zh, sm, zpad], 1), cm.T, sm.T)
    return ret, mla


def _tile(seq, want):
    t = min(seq, want)
    assert seq % t == 0, (seq, t)
    return t


def _trunk(x, p, depth):
    bsz, seq, _ = x.shape
    t = bsz * seq
    (cos_r, sin_r), mla_tabs = _rope_tables(seq)
    tm = _tile(seq, 1024)
    for l in range(depth):
        w = {k: v[l] for k, v in p.items()}
        x2 = x.reshape(t, D_MODEL)
        qk, rv, rg, lat, gates = _inproj(x2, w["g_mix"], w["w_in"], cos_r, sin_r, seq, tm)
        ret = _retention(w["dec"], w["ret_gn_g"],
                         qk.reshape(bsz, seq, -1), rv.reshape(bsz, seq, -1), rg.reshape(bsz, seq, -1),
                         _tile(seq, 2048))
        q, k, v = _mla_prep(lat.reshape(bsz, seq, -1), w, mla_tabs, _tile(seq, 1024))
        mla = _attention(w["att_shift"], q, k, v, _tile(seq, 1024), _tile(seq, 512))
        x1 = _merge(x2, ret.reshape(t, -1), mla.reshape(t, -1), gates,
                    w["w_ret_o"], w["w_mla_o"], w["w_out"], tm)
        x = _ffn(x1.reshape(bsz, seq, D_MODEL), w["g_ffn"], w["w_up"], w["conv_w"], w["conv_b"],
                 w["w_down"], _tile(seq, 512))
    return x


def _prepare_weights(g_mix, w_in, ret_decay_fwd, ret_decay_bwd, ret_gn_g, w_ret_o, g_cq, w_uq, g_ckv,
                     w_ukv, g_qn, g_kn, w_mla_o, w_out, g_ffn, w_up, conv_w, conv_b, w_down):
    depth = w_in.shape[0]
    kr_end = 2 * RET_Q_W + 2 * RET_V_W + Q_LORA + KV_LORA + MLA_ROPE
    w_in_p = jnp.concatenate(
        [w_in[:, :, :kr_end], jnp.zeros((depth, D_MODEL, LANES - MLA_ROPE), w_in.dtype), w_in[:, :, kr_end:]],
        axis=2).astype(BF16)
    pad_h = MLA_QK_PAD - MLA_QK
    w_uq_p = jnp.pad(w_uq.reshape(depth, Q_LORA, MLA_HEADS, MLA_QK), ((0, 0), (0, 0), (0, 0), (0, pad_h)))
    w_uq_t = jnp.swapaxes(w_uq_p.reshape(depth, Q_LORA, MLA_HEADS * MLA_QK_PAD), 1, 2).astype(BF16)
    w_ukv_h = w_ukv.reshape(depth, KV_LORA, MLA_HEADS, MLA_NOPE + MLA_V)
    w_k = w_ukv_h[..., :MLA_NOPE].reshape(depth, KV_LORA, MLA_HEADS * MLA_NOPE).astype(BF16)
    w_v_t = jnp.swapaxes(w_ukv_h[..., MLA_NOPE:].reshape(depth, KV_LORA, MLA_HEADS * MLA_V), 1, 2).astype(BF16)
    qscale = LOG2_E * MLA_QK ** -0.5
    g_qn_rows = jnp.broadcast_to((jnp.pad(g_qn, ((0, 0), (0, pad_h))) * qscale)[..., None],
                                 (depth, MLA_QK_PAD, LANES)).astype(F32)
    att_shift = (MLA_QK * qscale * jnp.max(jnp.abs(g_qn), axis=1) * jnp.max(jnp.abs(g_kn), axis=1))
    att_shift = att_shift.astype(F32).reshape(depth, 1, 1)
    dec = jnp.stack([ret_decay_fwd, ret_decay_bwd], axis=1).astype(F32)
    dec = jnp.broadcast_to(dec[..., None], (depth, 2, RET_HEADS, LANES))
    row = lambda a: a[:, None, :].astype(F32)
    return {
        "g_mix": row(g_mix), "w_in": w_in_p, "dec": dec, "ret_gn_g": row(ret_gn_g),
        "w_ret_o": w_ret_o.astype(BF16), "g_cq": row(g_cq), "w_uq_t": w_uq_t, "g_ckv": row(g_ckv),
        "w_k": w_k, "w_v_t": w_v_t, "att_shift": att_shift,
        "g_qn_rows": g_qn_rows, "g_kn": row(jnp.pad(g_kn, ((0, 0), (0, pad_h)))),
        "w_mla_o": w_mla_o.astype(BF16), "w_out": w_out.astype(BF16), "g_ffn": row(g_ffn),
        "w_up": w_up.astype(BF16), "conv_w": conv_w.astype(F32), "conv_b": row(conv_b),
        "w_down": w_down.astype(BF16),
    }


def kernel(x_prompt, x_sample, g_mix, w_in, ret_decay_fwd, ret_decay_bwd, ret_gn_g, w_ret_o, g_cq, w_uq,
           g_ckv, w_ukv, g_qn, g_kn, w_mla_o, w_out, g_ffn, w_up, conv_w, conv_b, w_down):
    depth = w_in.shape[0]
    p = _prepare_weights(g_mix, w_in, ret_decay_fwd, ret_decay_bwd, ret_gn_g, w_ret_o, g_cq, w_uq, g_ckv,
                         w_ukv, g_qn, g_kn, w_mla_o, w_out, g_ffn, w_up, conv_w, conv_b, w_down)
    return (_trunk(x_prompt, p, depth), _trunk(x_sample, p, depth))
```

```python
import functools

import jax
import jax.numpy as jnp
from jax import lax
from jax.experimental import pallas as pl
from jax.experimental.pallas import tpu as pltpu

D_MODEL = 1024
RET_HEADS = 4
RET_DK = 128
RET_DV = 256
RET_CHUNK = 128
MLA_HEADS = 8
MLA_NOPE = 128
MLA_ROPE = 64
MLA_V = 128
MLA_QK = MLA_NOPE + MLA_ROPE
MLA_QK_PAD = 256
Q_LORA = 384
KV_LORA = 256
D_FF = 2816
ROPE_BASE = 10000.0
EPS = 1e-6
LOG2_E = 1.4426950408889634

RET_Q_W = RET_HEADS * RET_DK
RET_V_W = RET_HEADS * RET_DV
LATENT_W = Q_LORA + KV_LORA + 128
IN_W_PAD = 2 * RET_Q_W + 2 * RET_V_W + LATENT_W + 2 * D_MODEL

LANES = 128
VMEM_LIMIT_BYTES = 56 * 1024 * 1024

F32 = jnp.float32
BF16 = jnp.bfloat16


def _params(*semantics):
    return pltpu.CompilerParams(dimension_semantics=semantics, vmem_limit_bytes=VMEM_LIMIT_BYTES)


def _resident(shape):
    return pl.BlockSpec(shape, lambda *_: (0,) * len(shape), pipeline_mode=pl.Buffered(1))


def _sigmoid(x):
    return 1.0 / (1.0 + jnp.exp(-x))


def _dot(a, b):
    return jnp.dot(a, b, preferred_element_type=F32)


def _dot_nt(a, b):
    return lax.dot_general(a, b, (((1,), (1,)), ((), ())), preferred_element_type=F32)


def _dot_tn(a, b):
    return lax.dot_general(a, b, (((0,), (0,)), ((), ())), preferred_element_type=F32)


def _inproj_kernel(x_ref, g_ref, w_ref, cos_ref, sin_ref,
                   qk_ref, v_ref, rg_ref, lat_ref, gates_ref):
    xf = x_ref[...]
    ms = jnp.mean(xf * xf, axis=-1, keepdims=True)
    h = (xf * lax.rsqrt(ms + EPS) * g_ref[...]).astype(BF16)
    cos = cos_ref[...]
    sin = sin_ref[...]

    def proj(c0, width):
        return _dot(h, w_ref[:, c0:c0 + width])

    for seg, scale in ((0, None), (1, RET_DK ** -0.5)):
        acc = proj(seg * RET_Q_W, RET_Q_W)
        for hd in range(RET_HEADS):
            xs = acc[:, hd * RET_DK:(hd + 1) * RET_DK]
            r = xs * cos + pltpu.roll(xs, RET_DK // 2, 1) * sin
            if scale is not None:
                r = r * scale
            c0 = seg * RET_Q_W + hd * RET_DK
            qk_ref[:, c0:c0 + RET_DK] = r.astype(BF16)
    base = 2 * RET_Q_W
    v_ref[...] = proj(base, RET_V_W).astype(BF16)
    base += RET_V_W
    a = proj(base, RET_V_W)
    rg_ref[...] = (a * _sigmoid(a)).astype(BF16)
    base += RET_V_W
    lat_ref[...] = proj(base, LATENT_W).astype(BF16)
    base += LATENT_W
    for j in range(2):
        a = proj(base + j * D_MODEL, D_MODEL)
        gates_ref[:, j * D_MODEL:(j + 1) * D_MODEL] = _sigmoid(a).astype(BF16)


def _inproj(x2, g_mix, w_in_p, cos_r, sin_r, seq, tm):
    t = x2.shape[0]
    nseq = seq // tm
    row = lambda w: pl.BlockSpec((tm, w), lambda i: (i, 0))
    pos = pl.BlockSpec((tm, LANES), lambda i: (i % nseq, 0))
    out_w = (2 * RET_Q_W, RET_V_W, RET_V_W, LATENT_W, 2 * D_MODEL)
    return pl.pallas_call(
        _inproj_kernel,
        grid=(t // tm,),
        in_specs=[row(D_MODEL), _resident((1, D_MODEL)), _resident((D_MODEL, IN_W_PAD)), pos, pos],
        out_specs=[row(w) for w in out_w],
        out_shape=[jax.ShapeDtypeStruct((t, w), BF16) for w in out_w],
        compiler_params=_params("arbitrary"),
        name="inproj",
    )(x2, g_mix, w_in_p, cos_r, sin_r)


_T_DMAT, _T_QF, _T_QB, _T_KF, _T_KB, _T_CF, _T_CB = range(7)


def _retention_kernel(dec_ref, gn_ref, qk_ref, v_ref, rg_ref, o_ref,
                      tab_ref, sf_ref, sb_ref, sball_ref, *, ts, nblk):
    c = RET_CHUNK
    ncb = ts // c
    b = pl.program_id(0)
    ph = pl.program_id(1)
    i = pl.program_id(2)

    @pl.when((b == 0) & (ph == 0) & (i == 0))
    def _tables():
        row = lax.broadcasted_iota(jnp.int32, (c, c), 0).astype(F32)
        col = lax.broadcasted_iota(jnp.int32, (c, c), 1).astype(F32)
        diff = row - col
        for hd in range(RET_HEADS):
            def log_sigmoid(d):
                return jnp.minimum(d, 0.0) - jnp.log1p(jnp.exp(-jnp.abs(d)))
            lgf = log_sigmoid(dec_ref[0, hd:hd + 1, :])
            lgb = log_sigmoid(dec_ref[1, hd:hd + 1, :])
            tab_ref[hd, _T_DMAT] = jnp.where(diff >= 0.0,
                                             jnp.exp(lgf * jnp.maximum(diff, 0.0)),
                                             jnp.exp(lgb * jnp.maximum(-diff, 0.0)))
            tab_ref[hd, _T_QF] = jnp.exp(lgf * (row + 1.0))
            tab_ref[hd, _T_QB] = jnp.exp(lgb * (c - row))
            tab_ref[hd, _T_KF] = jnp.exp(lgf * (c - 1.0 - row))
            tab_ref[hd, _T_KB] = jnp.exp(lgb * row)
            tab_ref[hd, _T_CF] = jnp.exp(lgf * (row * 0.0 + c))
            tab_ref[hd, _T_CB] = jnp.exp(lgb * (row * 0.0 + c))

    def chunk_decay(hd, which):
        t = tab_ref[hd, which]
        return jnp.concatenate([t, t], axis=1)

    @pl.when(ph == 0)
    def _backward_states():
        @pl.when(i == 0)
        def _():
            sb_ref[...] = jnp.zeros_like(sb_ref)
        blk = nblk - 1 - i
        for cb in reversed(range(ncb)):
            r0 = cb * c
            for hd in range(RET_HEADS):
                k = qk_ref[0, r0:r0 + c, RET_Q_W + hd * RET_DK:RET_Q_W + (hd + 1) * RET_DK]
                v = v_ref[0, r0:r0 + c, hd * RET_DV:(hd + 1) * RET_DV]
                st = sb_ref[hd]
                sball_ref[blk * ncb + cb, hd] = st.astype(BF16)
                kd = (k.astype(F32) * tab_ref[hd, _T_KB]).astype(BF16)
                sb_ref[hd] = st * chunk_decay(hd, _T_CB) + _dot_tn(kd, v)

    @pl.when(ph == 1)
    def _forward():
        @pl.when(i == 0)
        def _():
            sf_ref[...] = jnp.zeros_like(sf_ref)
        for cb in range(ncb):
            r0 = cb * c
            for hd in range(RET_HEADS):
                q = qk_ref[0, r0:r0 + c, hd * RET_DK:(hd + 1) * RET_DK]
                k = qk_ref[0, r0:r0 + c, RET_Q_W + hd * RET_DK:RET_Q_W + (hd + 1) * RET_DK]
                v = v_ref[0, r0:r0 + c, hd * RET_DV:(hd + 1) * RET_DV]
                qf32 = q.astype(F32)
                kf32 = k.astype(F32)
                sd = (_dot_nt(q, k) * tab_ref[hd, _T_DMAT]).astype(BF16)
                st = sf_ref[hd]
                out = _dot(sd, v)
                out = out + _dot((qf32 * tab_ref[hd, _T_QF]).astype(BF16), st.astype(BF16))
                out = out + _dot((qf32 * tab_ref[hd, _T_QB]).astype(BF16), sball_ref[i * ncb + cb, hd])
                kd = (kf32 * tab_ref[hd, _T_KF]).astype(BF16)
                sf_ref[hd] = st * chunk_decay(hd, _T_CF) + _dot_tn(kd, v)
                mu = jnp.mean(out, axis=-1, keepdims=True)
                xc = out - mu
                var = jnp.mean(xc * xc, axis=-1, keepdims=True)
                y = xc * lax.rsqrt(var + EPS) * gn_ref[:, hd * RET_DV:(hd + 1) * RET_DV]
                gate = rg_ref[0, r0:r0 + c, hd * RET_DV:(hd + 1) * RET_DV].astype(F32)
                o_ref[0, r0:r0 + c, hd * RET_DV:(hd + 1) * RET_DV] = (y * gate).astype(BF16)


def _retention(dec, gn_g, qk, v, rg, ts):
    bsz, seq, _ = qk.shape
    nblk = seq // ts
    nchunk = seq // RET_CHUNK
    sweep = lambda b, ph, i: (b, i * ph + (nblk - 1 - i) * (1 - ph), 0)
    fwd_only = lambda b, ph, i: (b, i * ph, 0)
    return pl.pallas_call(
        functools.partial(_retention_kernel, ts=ts, nblk=nblk),
        grid=(bsz, 2, nblk),
        in_specs=[_resident((2, RET_HEADS, LANES)), _resident((1, RET_V_W)),
                  pl.BlockSpec((1, ts, 2 * RET_Q_W), sweep),
                  pl.BlockSpec((1, ts, RET_V_W), sweep),
                  pl.BlockSpec((1, ts, RET_V_W), fwd_only)],
        out_specs=pl.BlockSpec((1, ts, RET_V_W), fwd_only),
        out_shape=jax.ShapeDtypeStruct((bsz, seq, RET_V_W), BF16),
        scratch_shapes=[pltpu.VMEM((RET_HEADS, 7, RET_CHUNK, RET_CHUNK), F32),
                        pltpu.VMEM((RET_HEADS, RET_DK, RET_DV), F32),
                        pltpu.VMEM((RET_HEADS, RET_DK, RET_DV), F32),
                        pltpu.VMEM((nchunk, RET_HEADS, RET_DK, RET_DV), BF16)],
        compiler_params=_params("arbitrary", "arbitrary", "arbitrary"),
        name="retention",
    )(dec, gn_g, qk, v, rg)


def _rope64(x, cos, sin_lo, sin_hi):
    return x * cos + pltpu.roll(x, LANES - MLA_ROPE // 2, 1) * sin_lo + pltpu.roll(x, MLA_ROPE // 2, 1) * sin_hi


def _mla_prep_kernel(lat_ref, gcq_ref, wuqt_ref, gckv_ref, wk_ref, wvt_ref, gqn_ref, gkn_ref,
                     cos_ref, slo_ref, shi_ref, cost_ref, sint_ref, qt_ref, k_ref, vt_ref):
    ts = lat_ref.shape[1]

    def rms(x, g):
        ms = jnp.mean(x * x, axis=-1, keepdims=True)
        return x * lax.rsqrt(ms + EPS) * g

    cqn = rms(lat_ref[0, :, :Q_LORA].astype(F32), gcq_ref[...])
    ckvn = rms(lat_ref[0, :, Q_LORA:Q_LORA + KV_LORA].astype(F32), gckv_ref[...])
    kr = lat_ref[0, :, Q_LORA + KV_LORA:].astype(F32)
    cqn_t = cqn.T.astype(BF16)
    ckvn_t = ckvn.T.astype(BF16)

    vt_all = _dot(wvt_ref[...], ckvn_t)
    for hd in range(MLA_HEADS):
        for c in range(ts // LANES):
            vt_ref[0, hd, c] = vt_all[hd * MLA_V:(hd + 1) * MLA_V, c * LANES:(c + 1) * LANES].astype(BF16)

    half = MLA_ROPE // 2
    r0 = MLA_NOPE
    for hd in range(MLA_HEADS):
        qh = _dot(wuqt_ref[hd * MLA_QK_PAD:(hd + 1) * MLA_QK_PAD, :], cqn_t)
        for c in range(ts // LANES):
            lanes = slice(c * LANES, (c + 1) * LANES)
            qc = qh[:, lanes]
            ssq = jnp.sum(qc * qc, axis=0, keepdims=True) * (1.0 / MLA_QK)
            qn = qc * lax.rsqrt(ssq + EPS) * gqn_ref[...]
            x1 = qn[r0:r0 + half]
            x2 = qn[r0 + half:r0 + 2 * half]
            cos = cost_ref[:, lanes]
            sin = sint_ref[:, lanes]
            qt_ref[0, hd, :r0, lanes] = qn[:r0].astype(BF16)
            qt_ref[0, hd, r0:r0 + half, lanes] = (x1 * cos - x2 * sin).astype(BF16)
            qt_ref[0, hd, r0 + half:r0 + 2 * half, lanes] = (x2 * cos + x1 * sin).astype(BF16)
            qt_ref[0, hd, r0 + 2 * half:, lanes] = jnp.zeros((MLA_QK_PAD - MLA_QK, LANES), BF16)

    cos = cos_ref[...]
    slo = slo_ref[...]
    shi = shi_ref[...]
    gkn = gkn_ref[...]
    kr = _rope64(kr, cos, slo, shi)
    kr_sq = jnp.sum(kr * kr, axis=-1, keepdims=True)
    kr_rot = _rope64(kr * gkn[:, MLA_NOPE:], cos, slo, shi)
    kn_all = _dot(ckvn.astype(BF16), wk_ref[...])
    for hd in range(MLA_HEADS):
        kn = kn_all[:, hd * MLA_NOPE:(hd + 1) * MLA_NOPE]
        ssk = (jnp.sum(kn * kn, axis=-1, keepdims=True) + kr_sq) * (1.0 / MLA_QK)
        rk = lax.rsqrt(ssk + EPS)
        k_ref[0, hd, :, :MLA_NOPE] = (kn * rk * gkn[:, :MLA_NOPE]).astype(BF16)
        k_ref[0, hd, :, MLA_NOPE:] = (kr_rot * rk).astype(BF16)


def _mla_prep(lat, w, tabs, ts):
    bsz, seq, _ = lat.shape
    cos_m, slo_m, shi_m, cos_t, sin_t = tabs
    pos = pl.BlockSpec((ts, LANES), lambda b, i: (i, 0))
    pos_t = pl.BlockSpec((MLA_ROPE // 2, ts), lambda b, i: (0, i))
    return pl.pallas_call(
        _mla_prep_kernel,
        grid=(bsz, seq // ts),
        in_specs=[pl.BlockSpec((1, ts, LATENT_W), lambda b, i: (b, i, 0)),
                  _resident((1, Q_LORA)), _resident((MLA_HEADS * MLA_QK_PAD, Q_LORA)),
                  _resident((1, KV_LORA)), _resident((KV_LORA, MLA_HEADS * MLA_NOPE)),
                  _resident((MLA_HEADS * MLA_V, KV_LORA)),
                  _resident((MLA_QK_PAD, LANES)), _resident((1, MLA_QK_PAD)),
                  pos, pos, pos, pos_t, pos_t],
        out_specs=[pl.BlockSpec((1, MLA_HEADS, MLA_QK_PAD, ts), lambda b, i: (b, 0, 0, i)),
                   pl.BlockSpec((1, MLA_HEADS, ts, MLA_QK_PAD), lambda b, i: (b, 0, i, 0)),
                   pl.BlockSpec((1, MLA_HEADS, ts // LANES, MLA_V, LANES), lambda b, i: (b, 0, i, 0, 0))],
        out_shape=[jax.ShapeDtypeStruct((bsz, MLA_HEADS, MLA_QK_PAD, seq), BF16),
                   jax.ShapeDtypeStruct((bsz, MLA_HEADS, seq, MLA_QK_PAD), BF16),
                   jax.ShapeDtypeStruct((bsz, MLA_HEADS, seq // LANES, MLA_V, LANES), BF16)],
        compiler_params=_params("arbitrary", "arbitrary"),
        name="mla_prep",
    )(lat, w["g_cq"], w["w_uq_t"], w["g_ckv"], w["w_k"], w["w_v_t"], w["g_qn_rows"], w["g_kn"],
      cos_m, slo_m, shi_m, cos_t, sin_t)


ATT_UNROLL = 16
ATT_UNROLL_RM = 4
ATT_PITCH_PAD = LANES
ATT_FIXED_SHIFT_MAX = 60.0


def _attention_kernel(shift_ref, qt_ref, k_ref, vt_ref, o_ref, s0_ref, s1_ref, *, tk, nkv):
    qt = qt_ref[0, 0]
    tq = qt.shape[1]
    sub = tk // LANES
    s_refs = (s0_ref, s1_ref)

    def scores(j):
        k0 = j * tk
        if not isinstance(j, int):
            k0 = pl.multiple_of(k0, tk)
        return _dot(k_ref[0, 0, pl.ds(k0, tk), :], qt)

    def values_t(j):
        vt3 = vt_ref[0, 0, pl.ds(j * sub, sub)]
        return jnp.concatenate([vt3[i] for i in range(sub)], axis=1)

    def blocks(n_trips, unroll, trip_fn, tail_fn, carry):
        carry = lax.fori_loop(0, n_trips, trip_fn, carry)
        for j in range(n_trips * unroll, nkv):
            carry = tail_fn(j, carry)
        return carry

    shift = shift_ref[0, 0]
    fixed_ok = shift <= ATT_FIXED_SHIFT_MAX

    @pl.when(fixed_ok)
    def _fixed_shift():
        def block(j, carry):
            l, acc = carry
            e = jnp.exp2(scores(j) - shift)
            return l + jnp.sum(e, axis=0, keepdims=True), acc + _dot(values_t(j), e.astype(BF16))

        def trip(i, carry):
            for u in range(ATT_UNROLL):
                carry = block(i * ATT_UNROLL + u, carry)
            return carry

        carry = (jnp.zeros((1, tq), F32), jnp.zeros((MLA_V, tq), F32))
        l, acc = blocks(nkv // ATT_UNROLL, ATT_UNROLL, trip, block, carry)
        o_ref[0] = (acc / l).T.astype(BF16)

    @pl.when(jnp.logical_not(fixed_ok))
    def _running_max():
        def update(s_ref, j, carry):
            m, l, acc = carry
            s = s_ref[:, :tq]
            m_new = jnp.maximum(m, jnp.max(s, axis=0, keepdims=True))
            p = jnp.exp2(s - m_new)
            alpha = jnp.exp2(m - m_new)
            l = alpha * l + jnp.sum(p, axis=0, keepdims=True)
            acc = alpha * acc + _dot(values_t(j), p.astype(BF16))
            return m_new, l, acc

        def trip(i, carry):
            j = i * ATT_UNROLL_RM
            for u in range(ATT_UNROLL_RM):
                s_refs[(u + 1) % 2][:, :tq] = scores(j + u + 1)
                carry = update(s_refs[u % 2], j + u, carry)
            return carry

        def tail(j, carry):
            if j + 1 < nkv:
                s_refs[(j + 1) % 2][:, :tq] = scores(j + 1)
            return update(s_refs[j % 2], j, carry)

        carry = (jnp.full((1, tq), -jnp.inf, F32), jnp.zeros((1, tq), F32), jnp.zeros((MLA_V, tq), F32))
        s0_ref[:, :tq] = scores(0)
        _, l, acc = blocks((nkv - 1) // ATT_UNROLL_RM, ATT_UNROLL_RM, trip, tail, carry)
        o_ref[0] = (acc / l).T.astype(BF16)


def _attention(shift, qt, k, vt, tq, tk):
    bsz, nh, seq, _ = k.shape
    nkv = seq // tk
    assert nkv >= 2 and nkv % 2 == 0 and tk % LANES == 0, (seq, tk)
    return pl.pallas_call(
        functools.partial(_attention_kernel, tk=tk, nkv=nkv),
        grid=(bsz, nh, seq // tq),
        in_specs=[pl.BlockSpec(memory_space=pltpu.SMEM),
                  pl.BlockSpec((1, 1, MLA_QK_PAD, tq), lambda b, h, i: (b, h, 0, i)),
                  pl.BlockSpec((1, 1, seq, MLA_QK_PAD), lambda b, h, i: (b, h, 0, 0)),
                  pl.BlockSpec((1, 1, seq // LANES, MLA_V, LANES), lambda b, h, i: (b, h, 0, 0, 0))],
        out_specs=pl.BlockSpec((1, tq, MLA_V), lambda b, h, i: (b, i, h)),
        out_shape=jax.ShapeDtypeStruct((bsz, seq, nh * MLA_V), BF16),
        scratch_shapes=[pltpu.VMEM((tk, tq + ATT_PITCH_PAD), F32), pltpu.VMEM((tk, tq + ATT_PITCH_PAD), F32)],
        compiler_params=_params("arbitrary", "arbitrary", "arbitrary"),
        name="attention",
    )(shift, qt, k, vt)


def _merge_kernel(x_ref, ret_ref, mla_ref, gates_ref, wr_ref, wm_ref, wo_ref, o_ref):
    ret_branch = _dot(ret_ref[...], wr_ref[...])
    mla_branch = _dot(mla_ref[...], wm_ref[...])
    merged = (gates_ref[:, :D_MODEL].astype(F32) * ret_branch
              + gates_ref[:, D_MODEL:].astype(F32) * mla_branch)
    o_ref[...] = x_ref[...] + _dot(merged.astype(BF16), wo_ref[...])


def _merge(x2, ret, mla, gates, w_ret_o, w_mla_o, w_out, tm):
    t = x2.shape[0]
    row = lambda w: pl.BlockSpec((tm, w), lambda i: (i, 0))
    sq = _resident((D_MODEL, D_MODEL))
    return pl.pallas_call(
        _merge_kernel,
        grid=(t // tm,),
        in_specs=[row(D_MODEL), row(RET_V_W), row(D_MODEL), row(2 * D_MODEL), sq, sq, sq],
        out_specs=row(D_MODEL),
        out_shape=jax.ShapeDtypeStruct((t, D_MODEL), F32),
        compiler_params=_params("arbitrary"),
        name="merge",
    )(x2, ret, mla, gates, w_ret_o, w_mla_o, w_out)


FFN_HALO = 8
FFN_NC = 256


def _ffn_kernel(x_ref, prev_ref, next_ref, g_ref, wup_ref, cw_ref, cb_ref, wdn_ref, o_ref, h_ref, act_ref,
                *, ts, nblk):
    i = pl.program_id(1)
    g = g_ref[...]

    def rms(x):
        ms = jnp.mean(x * x, axis=-1, keepdims=True)
        return x * lax.rsqrt(ms + EPS) * g

    keep_prev = jnp.where(i > 0, 1.0, 0.0)
    keep_next = jnp.where(i < nblk - 1, 1.0, 0.0)
    xm = x_ref[0]
    h_ref[:FFN_HALO] = (rms(prev_ref[0]) * keep_prev).astype(BF16)
    h_ref[FFN_HALO:FFN_HALO + ts] = rms(xm).astype(BF16)
    h_ref[FFN_HALO + ts:] = (rms(next_ref[0]) * keep_next).astype(BF16)
    h = h_ref[...]

    def conv(c0):
        u = _dot(h, wup_ref[:, c0:c0 + FFN_NC])
        w = cw_ref[:, c0:c0 + FFN_NC]
        rows = ts + 2 * FFN_HALO
        above = pltpu.roll(u, 1, 0)[FFN_HALO:FFN_HALO + ts]
        below = pltpu.roll(u, rows - 1, 0)[FFN_HALO:FFN_HALO + ts]
        return (above * w[0:1] + u[FFN_HALO:FFN_HALO + ts] * w[1:2] + below * w[2:3]
                + cb_ref[:, c0:c0 + FFN_NC])

    for n in range(D_FF // FFN_NC):
        ca = n * FFN_NC
        ua = conv(ca)
        ub = conv(D_FF + ca)
        act_ref[:, ca:ca + FFN_NC] = (ua * _sigmoid(ua) * ub).astype(BF16)
    o_ref[0] = xm + _dot(act_ref[...], wdn_ref[...])


def _ffn(x1, g_ffn, w_up, conv_w, conv_b, w_down, ts):
    bsz, seq, _ = x1.shape
    nblk = seq // ts
    hb = ts // FFN_HALO
    last_halo = seq // FFN_HALO - 1
    return pl.pallas_call(
        functools.partial(_ffn_kernel, ts=ts, nblk=nblk),
        grid=(bsz, nblk),
        in_specs=[pl.BlockSpec((1, ts, D_MODEL), lambda b, i: (b, i, 0)),
                  pl.BlockSpec((1, FFN_HALO, D_MODEL), lambda b, i: (b, jnp.maximum(i * hb - 1, 0), 0)),
                  pl.BlockSpec((1, FFN_HALO, D_MODEL), lambda b, i: (b, jnp.minimum((i + 1) * hb, last_halo), 0)),
                  _resident((1, D_MODEL)), _resident((D_MODEL, 2 * D_FF)),
                  _resident((3, 2 * D_FF)), _resident((1, 2 * D_FF)), _resident((D_FF, D_MODEL))],
        out_specs=pl.BlockSpec((1, ts, D_MODEL), lambda b, i: (b, i, 0)),
        out_shape=jax.ShapeDtypeStruct((bsz, seq, D_MODEL), F32),
        scratch_shapes=[pltpu.VMEM((ts + 2 * FFN_HALO, D_MODEL), BF16), pltpu.VMEM((ts, D_FF), BF16)],
        compiler_params=_params("arbitrary", "arbitrary"),
        name="ffn",
    )(x1, x1, x1, g_ffn, w_up, conv_w, conv_b, w_down)


def _rope_tables(seq):
    pos = jnp.arange(seq).astype(F32)[:, None]

    def cos_sin(d):
        inv = ROPE_BASE ** (-jnp.arange(0, d, 2, dtype=F32) / d)
        ang = pos * inv[None, :]
        return jnp.cos(ang), jnp.sin(ang)

    cr, sr = cos_sin(RET_DK)
    cm, sm = cos_sin(MLA_ROPE)
    zh = jnp.zeros_like(sm)
    zpad = jnp.zeros((seq, LANES - MLA_ROPE), F32)
    ret = (jnp.concatenate([cr, cr], 1), jnp.concatenate([-sr, sr], 1))
    mla = (jnp.concatenate([cm, cm, zpad], 1), jnp.concatenate([-sm, zh, zpad], 1),
           jnp.concatenate([zh, sm, zpad], 1), cm.T, sm.T)
    return ret, mla


def _tile(seq, want):
    t = min(seq, want)
    assert seq % t == 0, (seq, t)
    return t


def _trunk(x, p, depth, tables):
    bsz, seq, _ = x.shape
    t = bsz * seq
    (cos_r, sin_r), mla_tabs = tables
    tm = _tile(seq, 1024)
    for l in range(depth):
        w = {k: v[l] for k, v in p.items()}
        x2 = x.reshape(t, D_MODEL)
        qk, rv, rg, lat, gates = _inproj(x2, w["g_mix"], w["w_in"], cos_r, sin_r, seq, tm)
        ret = _retention(w["dec"], w["ret_gn_g"],
                         qk.reshape(bsz, seq, -1), rv.reshape(bsz, seq, -1), rg.reshape(bsz, seq, -1),
                         _tile(seq, 2048))
        q, k, v = _mla_prep(lat.reshape(bsz, seq, -1), w, mla_tabs, _tile(seq, 1024))
        mla = _attention(w["att_shift"], q, k, v, _tile(seq, 1024), _tile(seq, 2048))
        x1 = _merge(x2, ret.reshape(t, -1), mla.reshape(t, -1), gates,
                    w["w_ret_o"], w["w_mla_o"], w["w_out"], tm)
        x = _ffn(x1.reshape(bsz, seq, D_MODEL), w["g_ffn"], w["w_up"], w["conv_w"], w["conv_b"],
                 w["w_down"], _tile(seq, 512))
    return x


def _prepare_weights(g_mix, w_in, ret_decay_fwd, ret_decay_bwd, ret_gn_g, w_ret_o, g_cq, w_uq, g_ckv,
                     w_ukv, g_qn, g_kn, w_mla_o, w_out, g_ffn, w_up, conv_w, conv_b, w_down):
    depth = w_in.shape[0]
    kr_end = 2 * RET_Q_W + 2 * RET_V_W + Q_LORA + KV_LORA + MLA_ROPE
    w_in_p = jnp.concatenate(
        [w_in[:, :, :kr_end], jnp.zeros((depth, D_MODEL, LANES - MLA_ROPE), w_in.dtype), w_in[:, :, kr_end:]],
        axis=2).astype(BF16)
    pad_h = MLA_QK_PAD - MLA_QK
    w_uq_p = jnp.pad(w_uq.reshape(depth, Q_LORA, MLA_HEADS, MLA_QK), ((0, 0), (0, 0), (0, 0), (0, pad_h)))
    w_uq_t = jnp.swapaxes(w_uq_p.reshape(depth, Q_LORA, MLA_HEADS * MLA_QK_PAD), 1, 2).astype(BF16)
    w_ukv_h = w_ukv.reshape(depth, KV_LORA, MLA_HEADS, MLA_NOPE + MLA_V)
    w_k = w_ukv_h[..., :MLA_NOPE].reshape(depth, KV_LORA, MLA_HEADS * MLA_NOPE).astype(BF16)
    w_v_t = jnp.swapaxes(w_ukv_h[..., MLA_NOPE:].reshape(depth, KV_LORA, MLA_HEADS * MLA_V), 1, 2).astype(BF16)
    qscale = LOG2_E * MLA_QK ** -0.5
    g_qn_rows = jnp.broadcast_to((jnp.pad(g_qn, ((0, 0), (0, pad_h))) * qscale)[..., None],
                                 (depth, MLA_QK_PAD, LANES)).astype(F32)
    att_shift = (MLA_QK * qscale * jnp.max(jnp.abs(g_qn), axis=1) * jnp.max(jnp.abs(g_kn), axis=1))
    att_shift = att_shift.astype(F32).reshape(depth, 1, 1)
    dec = jnp.stack([ret_decay_fwd, ret_decay_bwd], axis=1).astype(F32)
    dec = jnp.broadcast_to(dec[..., None], (depth, 2, RET_HEADS, LANES))
    row = lambda a: a[:, None, :].astype(F32)
    return {
        "g_mix": row(g_mix), "w_in": w_in_p, "dec": dec, "ret_gn_g": row(ret_gn_g),
        "w_ret_o": w_ret_o.astype(BF16), "g_cq": row(g_cq), "w_uq_t": w_uq_t, "g_ckv": row(g_ckv),
        "w_k": w_k, "w_v_t": w_v_t, "att_shift": att_shift,
        "g_qn_rows": g_qn_rows, "g_kn": row(jnp.pad(g_kn, ((0, 0), (0, pad_h)))),
        "w_mla_o": w_mla_o.astype(BF16), "w_out": w_out.astype(BF16), "g_ffn": row(g_ffn),
        "w_up": w_up.astype(BF16), "conv_w": conv_w.astype(F32), "conv_b": row(conv_b),
        "w_down": w_down.astype(BF16),
    }


def kernel(x_prompt, x_sample, g_mix, w_in, ret_decay_fwd, ret_decay_bwd, ret_gn_g, w_ret_o, g_cq, w_uq,
           g_ckv, w_ukv, g_qn, g_kn, w_mla_o, w_out, g_ffn, w_up, conv_w, conv_b, w_down):
    depth = w_in.shape[0]
    p = _prepare_weights(g_mix, w_in, ret_decay_fwd, ret_decay_bwd, ret_gn_g, w_ret_o, g_cq, w_uq, g_ckv,
                         w_ukv, g_qn, g_kn, w_mla_o, w_out, g_ffn, w_up, conv_w, conv_b, w_down)
    tables = _rope_tables(max(x_prompt.shape[1], x_sample.shape[1]))
    return (_trunk(x_prompt, p, depth, tables), _trunk(x_sample, p, depth, tables))
```
